```python
import jax
import jax.numpy as jnp
from jax import lax
import numpy as np

D_MODEL = 1024
BATCH = 32
SEQ = 2048
DEPTH = 2
DEC_BATCH = 8
DEC_SEQ = 2048
PAST_LEN = 128

HEAD_DIM = 64
A_HEADS = 8
A_KV_HEADS = 2
B_HEADS = 4
B_KV_HEADS = 2
C_HEADS = 4
A_WIDTH = A_HEADS * HEAD_DIM
B_WIDTH = B_HEADS * HEAD_DIM
C_WIDTH = C_HEADS * HEAD_DIM
MIX_WIDTH = A_WIDTH + B_WIDTH + C_WIDTH
IN_SPLITS = (A_HEADS * HEAD_DIM, A_KV_HEADS * HEAD_DIM, A_KV_HEADS * HEAD_DIM,
             B_HEADS * HEAD_DIM, B_KV_HEADS * HEAD_DIM, B_KV_HEADS * HEAD_DIM,
             C_HEADS * HEAD_DIM, C_HEADS * HEAD_DIM, C_HEADS * HEAD_DIM)
IN_WIDTH = (A_HEADS + 2 * A_KV_HEADS + B_HEADS + 2 * B_KV_HEADS + 3 * C_HEADS) * HEAD_DIM
WINDOW = 128
BLOCK = 128
GRID_W = 64
NA_ROWS = 8
NA_COLS = 16
N_EXPERTS = 16
CAPACITY_FACTOR = 2
D_FF_EXPERT = 1024
ROPE_THETA = 10000.0
EPS = 1e-6
NEG_INF = -1e30

kernel_name = "hybrid_parallel_heads_ec_encoder"


def rmsnorm(x, g):
    xf = x.astype(jnp.float32)
    y = xf * lax.rsqrt(jnp.mean(xf * xf, axis=-1, keepdims=True) + EPS)
    return (y * g.astype(jnp.float32)).astype(x.dtype)


def rope_cos_sin(pos, dim):
    inv = ROPE_THETA ** (-jnp.arange(0, dim, 2, dtype=jnp.float32) / dim)
    ang = pos.astype(jnp.float32)[:, None] * inv[None, :]
    ang = jnp.concatenate([ang, ang], axis=-1)
    return jnp.cos(ang), jnp.sin(ang)


def apply_rope(x, cos, sin):
    xf = x.astype(jnp.float32)
    x1, x2 = jnp.split(xf, 2, axis=-1)
    rot = jnp.concatenate([-x2, x1], axis=-1)
    return (xf * cos[:, None, :] + rot * sin[:, None, :]).astype(x.dtype)


def apply_axial_rope(x):
    s = x.shape[1]
    t = jnp.arange(s)
    half = x.shape[-1] // 2
    cr, sr = rope_cos_sin(t // GRID_W, half)
    cc, sc = rope_cos_sin(t % GRID_W, half)
    xr, xc = jnp.split(x, 2, axis=-1)
    return jnp.concatenate([apply_rope(xr, cr, sr), apply_rope(xc, cc, sc)], axis=-1)


def window_sink_attention(q, k, v, sink):
    b, s, hq, dh = q.shape
    kvh = k.shape[2]
    g = hq // kvh
    q = q.reshape(b, s, kvh, g, dh)
    pad = ((0, 0), (WINDOW, WINDOW), (0, 0), (0, 0))
    kp = jnp.pad(k, pad)
    vp = jnp.pad(v, pad)
    span = BLOCK + 2 * WINDOW
    scale = dh ** -0.5
    sink_l = sink.astype(jnp.float32).reshape(kvh, g)

    def one_block(i):
        start = i * BLOCK
        qb = lax.dynamic_slice_in_dim(q, start, BLOCK, axis=1)
        kb = lax.dynamic_slice_in_dim(kp, start, span, axis=1)
        vb = lax.dynamic_slice_in_dim(vp, start, span, axis=1)
        qpos = start + jnp.arange(BLOCK)
        kpos = start - WINDOW + jnp.arange(span)
        ok = (jnp.abs(qpos[:, None] - kpos[None, :]) <= WINDOW) & (kpos[None, :] >= 0) & (kpos[None, :] < s)
        sc = jnp.einsum('bqkgd,bnkd->bkgqn', qb, kb).astype(jnp.float32) * scale
        sc = jnp.where(ok, sc, NEG_INF)
        sk = jnp.broadcast_to(sink_l[None, :, :, None, None], sc.shape[:-1] + (1,))
        p = jax.nn.softmax(jnp.concatenate([sc, sk], axis=-1), axis=-1)[..., :span]
        o = jnp.einsum('bkgqn,bnkd->bqkgd', p.astype(v.dtype), vb)
        return o.reshape(b, BLOCK, hq * dh)

    out = lax.map(one_block, jnp.arange(s // BLOCK))
    return jnp.moveaxis(out, 0, 1).reshape(b, s, hq * dh)


def dense_block_attention(q, k, v):
    b, s, hq, dh = q.shape
    kvh = k.shape[2]
    g = hq // kvh
    q = q.reshape(b, s, kvh, g, dh)
    scale = dh ** -0.5

    def one_block(i):
        qb = lax.dynamic_slice_in_dim(q, i * BLOCK, BLOCK, axis=1)
        sc = jnp.einsum('bqkgd,bnkd->bkgqn', qb, k).astype(jnp.float32) * scale
        p = jax.nn.softmax(sc, axis=-1)
        o = jnp.einsum('bkgqn,bnkd->bqkgd', p.astype(v.dtype), v)
        return o.reshape(b, BLOCK, hq * dh)

    out = lax.map(one_block, jnp.arange(s // BLOCK))
    return jnp.moveaxis(out, 0, 1).reshape(b, s, hq * dh)


def neighbourhood_attention(q, k, v, rpb):
    b, s, h, dh = q.shape
    rows = s // GRID_W
    kh = min(NA_ROWS, rows)
    kw = NA_COLS
    n_keys = kh * kw
    r = jnp.arange(rows)
    c = jnp.arange(GRID_W)
    rs = jnp.clip(r - kh // 2, 0, rows - kh)
    cs = jnp.clip(c - kw // 2, 0, GRID_W - kw)
    krow = rs[:, None] + jnp.arange(kh)[None, :]
    kcol = cs[:, None] + jnp.arange(kw)[None, :]
    idx = (krow[:, None, :, None] * GRID_W + kcol[None, :, None, :]).reshape(rows, GRID_W, n_keys)
    dr = krow - r[:, None] + (NA_ROWS - 1)
    dc = kcol - c[:, None] + (NA_COLS - 1)
    bias = rpb[:, dr[:, None, :, None], dc[None, :, None, :]]
    bias = jnp.moveaxis(bias.reshape(h, rows, GRID_W, n_keys), 1, 0)
    scale = dh ** -0.5

    def one_row(args):
        ri, idx_r, bias_r = args
        qr = lax.dynamic_slice_in_dim(q, ri * GRID_W, GRID_W, axis=1)
        flat = idx_r.reshape(-1)
        kg = jnp.take(k, flat, axis=1).reshape(b, GRID_W, n_keys, h, dh)
        vg = jnp.take(v, flat, axis=1).reshape(b, GRID_W, n_keys, h, dh)
        sc = jnp.einsum('bqhd,bqnhd->bhqn', qr, kg).astype(jnp.float32) * scale
        sc = sc + bias_r[None].astype(jnp.float32)
        p = jax.nn.softmax(sc, axis=-1)
        o = jnp.einsum('bhqn,bqnhd->bqhd', p.astype(v.dtype), vg)
        return o.reshape(b, GRID_W, h * dh)

    out = lax.map(one_row, (jnp.arange(rows), idx, bias))
    return jnp.moveaxis(out, 0, 1).reshape(b, s, h * dh)


def token_mixers(hn, w_in, sink, q_gain, k_gain, rpb, g_grp, w_out):
    b, s, _ = hn.shape
    proj = jnp.dot(hn, w_in)
    offsets = np.cumsum(IN_SPLITS)[:-1].tolist()
    qa, ka, va, qb, kb, vb, qc, kc, vc = jnp.split(proj, offsets, axis=-1)
    heads = lambda t: t.reshape(b, s, -1, HEAD_DIM)
    cos, sin = rope_cos_sin(jnp.arange(s), HEAD_DIM)
    oa = window_sink_attention(apply_rope(heads(qa), cos, sin), apply_rope(heads(ka), cos, sin), heads(va), sink)
    qb_ = apply_axial_rope(rmsnorm(heads(qb), q_gain))
    kb_ = apply_axial_rope(rmsnorm(heads(kb), k_gain))
    ob = dense_block_attention(qb_, kb_, heads(vb))
    oc = neighbourhood_attention(heads(qc), heads(kc), heads(vc), rpb)
    merged = jnp.concatenate([
        rmsnorm(oa, g_grp[:A_WIDTH]),
        rmsnorm(ob, g_grp[A_WIDTH:A_WIDTH + B_WIDTH]),
        rmsnorm(oc, g_grp[A_WIDTH + B_WIDTH:]),
    ], axis=-1)
    return jnp.dot(merged, w_out)


def expert_choice_ffn(hn, w_router, w_gate, w_up, w_down):
    b, s, d = hn.shape
    n = b * s
    cap = CAPACITY_FACTOR * n // N_EXPERTS
    hf = hn.reshape(n, d)
    aff = jax.nn.softmax(jnp.dot(hf, w_router).astype(jnp.float32), axis=-1)
    gate, idx = lax.top_k(aff.T, cap)
    xe = jnp.take(hf, idx, axis=0)
    hid = jax.nn.silu(jnp.einsum('ecd,edf->ecf', xe, w_gate)) * jnp.einsum('ecd,edf->ecf', xe, w_up)
    ye = jnp.einsum('ecf,efd->ecd', hid, w_down) * gate[..., None].astype(hn.dtype)
    y = jnp.zeros((n, d), hn.dtype).at[idx.reshape(-1)].add(ye.reshape(-1, d))
    return y.reshape(b, s, d)


def encoder_layer(x, c, w_ada, b_ada, g_mix, w_in, sink, q_gain, k_gain, rpb, g_grp, w_out,
                  g_ffn, w_router, w_gate, w_up, w_down):
    mod = jnp.dot(jax.nn.silu(c), w_ada) + b_ada
    sh1, sc1, gt1, sh2, sc2, gt2 = [m[:, None, :] for m in jnp.split(mod, 6, axis=-1)]
    hn = rmsnorm(x, g_mix) * (1 + sc1) + sh1
    x = x + gt1 * token_mixers(hn, w_in, sink, q_gain, k_gain, rpb, g_grp, w_out)
    hn = rmsnorm(x, g_ffn) * (1 + sc2) + sh2
    x = x + gt2 * expert_choice_ffn(hn, w_router, w_gate, w_up, w_down)
    return x


def setup_inputs(seed: int = 0) -> dict:
    key = jax.random.key(seed)
    ks = jax.random.split(key, 24)
    f32 = jnp.float32
    nrm = lambda k, shape, scale: jax.random.normal(k, shape, f32) * scale
    gain = lambda k, shape: 1.0 + 0.02 * jax.random.normal(k, shape, f32)
    D = D_MODEL
    return {
        "x_prompt": nrm(ks[0], (BATCH, SEQ, D), 1.0),
        "x_sample": nrm(ks[1], (DEC_BATCH, DEC_SEQ, D), 1.0),
        "c_prompt": nrm(ks[2], (BATCH, D), 1.0),
        "c_sample": nrm(ks[3], (DEC_BATCH, D), 1.0),
        "w_ada": nrm(ks[4], (DEPTH, D, 6 * D), D ** -0.5),
        "b_ada": nrm(ks[5], (DEPTH, 6 * D), 0.02),
        "g_mix": gain(ks[6], (DEPTH, D)),
        "w_in": nrm(ks[7], (DEPTH, D, IN_WIDTH), D ** -0.5),
        "sink": nrm(ks[8], (DEPTH, A_HEADS), 1.0),
        "q_gain": gain(ks[9], (DEPTH, HEAD_DIM)),
        "k_gain": gain(ks[10], (DEPTH, HEAD_DIM)),
        "rpb": nrm(ks[11], (DEPTH, C_HEADS, 2 * NA_ROWS - 1, 2 * NA_COLS - 1), 0.5),
        "g_grp": gain(ks[12], (DEPTH, MIX_WIDTH)),
        "w_out": nrm(ks[13], (DEPTH, MIX_WIDTH, D), MIX_WIDTH ** -0.5),
        "g_ffn": gain(ks[14], (DEPTH, D)),
        "w_router": nrm(ks[15], (DEPTH, D, N_EXPERTS), D ** -0.5),
        "w_gate": nrm(ks[16], (DEPTH, N_EXPERTS, D, D_FF_EXPERT), D ** -0.5),
        "w_up": nrm(ks[17], (DEPTH, N_EXPERTS, D, D_FF_EXPERT), D ** -0.5),
        "w_down": nrm(ks[18], (DEPTH, N_EXPERTS, D_FF_EXPERT, D), D_FF_EXPERT ** -0.5),
        "g_final": gain(ks[19], (D,)),
    }


def reference(x_prompt, x_sample, c_prompt, c_sample, w_ada, b_ada, g_mix, w_in, sink, q_gain, k_gain,
              rpb, g_grp, w_out, g_ffn, w_router, w_gate, w_up, w_down, g_final):
    def trunk(x, c):
        for i in range(DEPTH):
            x = encoder_layer(x, c, w_ada[i], b_ada[i], g_mix[i], w_in[i], sink[i], q_gain[i], k_gain[i],
                              rpb[i], g_grp[i], w_out[i], g_ffn[i], w_router[i], w_gate[i], w_up[i], w_down[i])
        return rmsnorm(x, g_final)

    y_prompt = trunk(x_prompt, c_prompt)
    y_sample = trunk(x_sample, c_sample)
    return (y_prompt, y_sample)
```

```python
import functools

import numpy as np
import jax
import jax.numpy as jnp
from jax import lax
from jax.experimental import pallas as pl
from jax.experimental.pallas import tpu as pltpu

F32 = jnp.float32
BF16 = jnp.bfloat16
I32 = jnp.int32

D_MODEL = 1024
SEQ = 2048
HEAD_DIM = 64
A_HEADS, A_KV = 8, 2
B_HEADS, B_KV = 4, 2
C_HEADS = 4
A_W, B_W, C_W = A_HEADS * HEAD_DIM, B_HEADS * HEAD_DIM, C_HEADS * HEAD_DIM
IN_WIDTH = 2048
WINDOW = 128
GRID_W = 64
GRID_ROWS = SEQ // GRID_W
NA_ROWS, NA_COLS = 8, 16
N_EXPERTS = 16
CAPACITY_FACTOR = 2
ROPE_THETA = 10000.0
EPS = 1e-6
NEG_INF = -1e30
Q_SCALE = HEAD_DIM ** -0.5

LANES = 128
SUBLANES = 8
SUBLANE_SHIFT = 3
TOK_TILE = 512
Q_TILE_B = 256
ROUTE_TILE = 256
CHUNK = 64
CHUNK_SHIFT = 6
SLACK = 16
WIN = CHUNK + SLACK
FFN_TILE = 512
VMEM_LIMIT = 56 * 1024 * 1024


def _cparams(*sem):
    return pltpu.CompilerParams(dimension_semantics=sem, vmem_limit_bytes=VMEM_LIMIT)


def _split_bf16(a):
    hi = a.astype(BF16)
    lo = (a - hi.astype(F32)).astype(BF16)
    return hi, lo


def _dot(a, b):
    return jnp.dot(a, b, preferred_element_type=F32)


def _dot_nt(a, b):
    return lax.dot_general(a, b, (((1,), (1,)), ((), ())), preferred_element_type=F32)


def _dot_tn(a, b):
    return lax.dot_general(a, b, (((0,), (0,)), ((), ())), preferred_element_type=F32)


def _dot3(a, w):
    ah, al = _split_bf16(a)
    wh, wl = _split_bf16(w)
    return _dot(ah, wh) + _dot(al, wh) + _dot(ah, wl)


def _rms(x):
    return x * lax.rsqrt(jnp.mean(x * x, axis=-1, keepdims=True) + EPS)


def _ada_kernel(c_ref, w_ref, b_ref, o_ref):
    c = c_ref[...]
    a = c * jax.nn.sigmoid(c)
    o_ref[...] = _dot3(a, w_ref[...]) + b_ref[...]


def _ada(c, w_ada, b_ada):
    b = c.shape[0]
    return pl.pallas_call(
        _ada_kernel,
        grid=(6,),
        in_specs=[
            pl.BlockSpec((b, D_MODEL), lambda j: (0, 0)),
            pl.BlockSpec((D_MODEL, D_MODEL), lambda j: (0, j)),
            pl.BlockSpec((1, D_MODEL), lambda j: (0, j)),
        ],
        out_specs=pl.BlockSpec((b, D_MODEL), lambda j: (0, j)),
        out_shape=jax.ShapeDtypeStruct((b, 6 * D_MODEL), F32),
        compiler_params=_cparams("arbitrary"),
        name="ada",
    )(c, w_ada, b_ada.reshape(1, 6 * D_MODEL))


def _rotate(x, cos, sin_signed, half):
    w = x.shape[1]
    lane = lax.broadcasted_iota(I32, x.shape, 1)
    first = (lane % (2 * half)) < half
    rot = jnp.where(first, pltpu.roll(x, w - half, 1), pltpu.roll(x, half, 1))
    return x * cos + rot * sin_signed


def _head_rms(x, hm, gain):
    sq = x * x
    sh, sl = _split_bf16(sq)
    ms = (_dot(sh, hm) + _dot(sl, hm)) * (1.0 / HEAD_DIM)
    return x * lax.rsqrt(ms + EPS) * gain


def _inproj_kernel(x_ref, mod_ref, g_ref, w_ref, ca_ref, sa_ref, cb_ref, sb_ref, qg_ref, kg_ref, hm_ref,
                   qa_ref, ka_ref, va_ref, qb_ref, kb_ref, vb_ref, qc_ref, kc_ref, vc_ref):
    x = x_ref[0]
    hn = _rms(x) * g_ref[...]
    hn = hn * (1.0 + mod_ref[0, 1:2, :]) + mod_ref[0, 0:1, :]
    proj = _dot(hn.astype(BF16), w_ref[...])
    ca, sa, cb, sb = ca_ref[...], sa_ref[...], cb_ref[...], sb_ref[...]
    hm = hm_ref[...]

    col = 0
    for j in range(A_W // LANES):
        qa_ref[0, :, j * LANES:(j + 1) * LANES] = (
            _rotate(proj[:, col:col + LANES], ca, sa, HEAD_DIM // 2) * Q_SCALE).astype(BF16)
        col += LANES
    ka_ref[0] = _rotate(proj[:, col:col + LANES], ca, sa, HEAD_DIM // 2).astype(BF16)
    col += LANES
    va_ref[0] = proj[:, col:col + LANES].astype(BF16)
    col += LANES
    for j in range(B_W // LANES):
        q = _head_rms(proj[:, col:col + LANES], hm, qg_ref[...])
        qb_ref[0, :, j * LANES:(j + 1) * LANES] = (_rotate(q, cb, sb, HEAD_DIM // 4) * Q_SCALE).astype(BF16)
        col += LANES
    k = _head_rms(proj[:, col:col + LANES], hm, kg_ref[...])
    kb_ref[0] = _rotate(k, cb, sb, HEAD_DIM // 4).astype(BF16)
    col += LANES
    vb_ref[0] = proj[:, col:col + LANES].astype(BF16)
    col += LANES
    qc_ref[0] = (proj[:, col:col + C_W] * Q_SCALE).astype(BF16)
    col += C_W
    kc_ref[0] = proj[:, col:col + C_W].astype(BF16)
    col += C_W
    vc_ref[0] = proj[:, col:col + C_W].astype(BF16)


def _rope_tables():
    t = np.arange(SEQ, dtype=np.float64)[:, None]
    j = np.arange(LANES)[None, :] % HEAD_DIM
    inv_a = ROPE_THETA ** (-(2.0 * (j % 32)) / HEAD_DIM)
    ang_a = t * inv_a
    sign_a = np.where(j < 32, -1.0, 1.0)
    inv_b = ROPE_THETA ** (-(2.0 * (j % 16)) / (HEAD_DIM // 2))
    pos_b = np.where(j < 32, np.floor(t / GRID_W), t % GRID_W)
    ang_b = pos_b * inv_b
    sign_b = np.where((j % 32) < 16, -1.0, 1.0)
    f = lambda a: jnp.asarray(a, dtype=F32)
    return f(np.cos(ang_a)), f(np.sin(ang_a) * sign_a), f(np.cos(ang_b)), f(np.sin(ang_b) * sign_b)


def _inproj(x, mod, g_mix, w_in, q_gain, k_gain):
    b = x.shape[0]
    ca, sa, cb, sb = _rope_tables()
    hm = jnp.asarray(np.kron(np.eye(LANES // HEAD_DIM), np.ones((HEAD_DIM, HEAD_DIM))), dtype=BF16)
    gain2 = lambda g: jnp.tile(g, LANES // HEAD_DIM).reshape(1, LANES)
    tok = lambda w: pl.BlockSpec((1, TOK_TILE, w), lambda i, j: (i, j, 0))
    tab = pl.BlockSpec((TOK_TILE, LANES), lambda i, j: (j, 0))
    full = lambda r, c: pl.BlockSpec((r, c), lambda i, j: (0, 0))
    widths = (A_W, LANES, LANES, B_W, LANES, LANES, C_W, C_W, C_W)
    return pl.pallas_call(
        _inproj_kernel,
        grid=(b, SEQ // TOK_TILE),
        in_specs=[
            tok(D_MODEL),
            pl.BlockSpec((1, 6, D_MODEL), lambda i, j: (i, 0, 0)),
            full(1, D_MODEL),
            full(D_MODEL, IN_WIDTH),
            tab, tab, tab, tab,
            full(1, LANES), full(1, LANES), full(LANES, LANES),
        ],
        out_specs=[tok(w) for w in widths],
        out_shape=[jax.ShapeDtypeStruct((b, SEQ, w), BF16) for w in widths],
        compiler_params=_cparams("arbitrary", "arbitrary"),
        name="inproj",
    )(x, mod, g_mix.reshape(1, D_MODEL), w_in, ca, sa, cb, sb, gain2(q_gain), gain2(k_gain), hm)


def _softmax_pv(s, v, extra=None):
    m = jnp.max(s, axis=-1, keepdims=True)
    if extra is not None:
        m = jnp.maximum(m, extra)
    p = jnp.exp(s - m)
    l = jnp.sum(p, axis=-1, keepdims=True)
    if extra is not None:
        l = l + jnp.exp(extra - m)
    return _dot(p.astype(BF16), v) * (1.0 / l)


def _attn_a_kernel(sink_ref, q_ref, k_ref, v_ref, g_ref, o_ref):
    span = WINDOW * 3
    qi = lax.broadcasted_iota(I32, (WINDOW, span), 0)
    ki = lax.broadcasted_iota(I32, (WINDOW, span), 1)
    g = g_ref[...]
    group = A_HEADS // A_KV

    def block(i, carry):
        start = pl.multiple_of(i * WINDOW, WINDOW)
        kstart = pl.multiple_of(jnp.clip(start - WINDOW, 0, SEQ - span), WINDOW)
        q = q_ref[0, pl.ds(start, WINDOW), :]
        k = k_ref[0, pl.ds(kstart, span), :]
        v = v_ref[0, pl.ds(kstart, span), :]
        ok = jnp.abs(ki - qi + (kstart - start)) <= WINDOW
        outs = []
        for h in range(A_HEADS):
            kv = h // group
            s = _dot_nt(q[:, h * HEAD_DIM:(h + 1) * HEAD_DIM], k[:, kv * HEAD_DIM:(kv + 1) * HEAD_DIM])
            s = jnp.where(ok, s, NEG_INF)
            outs.append(_softmax_pv(s, v[:, kv * HEAD_DIM:(kv + 1) * HEAD_DIM], extra=sink_ref[h]))
        o = jnp.concatenate(outs, axis=1)
        o_ref[0, pl.ds(start, WINDOW), :] = (_rms(o) * g).astype(BF16)
        return carry

    lax.fori_loop(0, SEQ // WINDOW, block, 0)


def _attn_a(q, k, v, sink, g):
    b = q.shape[0]
    seq = lambda w: pl.BlockSpec((1, SEQ, w), lambda i: (i, 0, 0))
    return pl.pallas_call(
        _attn_a_kernel,
        grid=(b,),
        in_specs=[
            pl.BlockSpec(memory_space=pltpu.SMEM),
            seq(A_W), seq(LANES), seq(LANES),
            pl.BlockSpec((1, A_W), lambda i: (0, 0)),
        ],
        out_specs=seq(A_W),
        out_shape=jax.ShapeDtypeStruct((b, SEQ, A_W), BF16),
        compiler_params=_cparams("arbitrary"),
        name="attn_window",
    )(sink, q, k, v, g.reshape(1, A_W))


def _attn_b_kernel(q_ref, k_ref, v_ref, g_ref, o_ref):
    q = q_ref[0]
    k = k_ref[0]
    v = v_ref[0]
    group = B_HEADS // B_KV
    outs = []
    for h in range(B_HEADS):
        kv = h // group
        s = _dot_nt(q[:, h * HEAD_DIM:(h + 1) * HEAD_DIM], k[:, kv * HEAD_DIM:(kv + 1) * HEAD_DIM])
        outs.append(_softmax_pv(s, v[:, kv * HEAD_DIM:(kv + 1) * HEAD_DIM]))
    o = jnp.concatenate(outs, axis=1)
    o_ref[0] = (_rms(o) * g_ref[...]).astype(BF16)


def _attn_b(q, k, v, g):
    b = q.shape[0]
    seq = lambda w: pl.BlockSpec((1, SEQ, w), lambda i, j: (i, 0, 0))
    til = pl.BlockSpec((1, Q_TILE_B, B_W), lambda i, j: (i, j, 0))
    return pl.pallas_call(
        _attn_b_kernel,
        grid=(b, SEQ // Q_TILE_B),
        in_specs=[til, seq(LANES), seq(LANES), pl.BlockSpec((1, B_W), lambda i, j: (0, 0))],
        out_specs=til,
        out_shape=jax.ShapeDtypeStruct((b, SEQ, B_W), BF16),
        compiler_params=_cparams("arbitrary", "arbitrary"),
        name="attn_dense",
    )(q, k, v, g.reshape(1, B_W))


def _attn_c_kernel(q_ref, k_ref, v_ref, bias_ref, g_ref, o_ref):
    nkeys = NA_ROWS * GRID_W
    g = g_ref[...]

    def row(r, carry):
        rs = jnp.clip(r - NA_ROWS // 2, 0, GRID_ROWS - NA_ROWS)
        cls = r - rs
        qstart = pl.multiple_of(r * GRID_W, GRID_W)
        kstart = pl.multiple_of(rs * GRID_W, GRID_W)
        q = q_ref[0, pl.ds(qstart, GRID_W), :]
        k = k_ref[0, pl.ds(kstart, nkeys), :]
        v = v_ref[0, pl.ds(kstart, nkeys), :]
        outs = []
        for h in range(C_HEADS):
            sl = slice(h * HEAD_DIM, (h + 1) * HEAD_DIM)
            s = _dot_nt(q[:, sl], k[:, sl]) + bias_ref[cls, h]
            outs.append(_softmax_pv(s, v[:, sl]))
        o = jnp.concatenate(outs, axis=1)
        o_ref[0, pl.ds(qstart, GRID_W), :] = (_rms(o) * g).astype(BF16)
        return carry

    lax.fori_loop(0, GRID_ROWS, row, 0)


def _nbr_bias(rpb):
    cls = np.arange(NA_ROWS)[:, None, None, None]
    c = np.arange(GRID_W)[None, :, None, None]
    kr = np.arange(NA_ROWS)[None, None, :, None]
    kc = np.arange(GRID_W)[None, None, None, :]
    cs = np.clip(c - NA_COLS // 2, 0, GRID_W - NA_COLS)
    inside = (kc >= cs) & (kc < cs + NA_COLS)
    dr = np.broadcast_to(kr - cls + (NA_ROWS - 1), (NA_ROWS, GRID_W, NA_ROWS, GRID_W))
    dc = np.broadcast_to(np.clip(kc - c + (NA_COLS - 1), 0, 2 * NA_COLS - 2), dr.shape)
    inside = np.broadcast_to(inside, dr.shape)
    bias = rpb[:, dr, dc]
    bias = jnp.where(inside[None], bias, NEG_INF)
    return jnp.moveaxis(bias, 0, 1).reshape(NA_ROWS, C_HEADS, GRID_W, NA_ROWS * GRID_W)


def _attn_c(q, k, v, rpb, g):
    b = q.shape[0]
    seq = pl.BlockSpec((1, SEQ, C_W), lambda i: (i, 0, 0))
    return pl.pallas_call(
        _attn_c_kernel,
        grid=(b,),
        in_specs=[
            seq, seq, seq,
            pl.BlockSpec((NA_ROWS, C_HEADS, GRID_W, NA_ROWS * GRID_W), lambda i: (0, 0, 0, 0)),
            pl.BlockSpec((1, C_W), lambda i: (0, 0)),
        ],
        out_specs=seq,
        out_shape=jax.ShapeDtypeStruct((b, SEQ, C_W), BF16),
        compiler_params=_cparams("arbitrary"),
        name="attn_nbr",
    )(q, k, v, _nbr_bias(rpb), g.reshape(1, C_W))


def _outproj_kernel(x_ref, ma_ref, mb_ref, mc_ref, mod_ref, wo_ref, g_ref, wr_ref, x1_ref, hn_ref, aff_ref):
    attn = (_dot(ma_ref[0], wo_ref[0:A_W, :]) + _dot(mb_ref[0], wo_ref[A_W:A_W + B_W, :])
            + _dot(mc_ref[0], wo_ref[A_W + B_W:, :]))
    x1 = x_ref[0] + mod_ref[0, 2:3, :] * attn
    x1_ref[0] = x1
    hn = _rms(x1) * g_ref[...]
    hn = hn * (1.0 + mod_ref[0, 4:5, :]) + mod_ref[0, 3:4, :]
    hn_ref[...] = hn.astype(BF16)
    logits = _dot3(hn, wr_ref[...])
    lt = logits.T[0:N_EXPERTS, :]
    e = jnp.exp(lt - jnp.max(lt, axis=0, keepdims=True))
    aff_ref[...] = e * (1.0 / jnp.sum(e, axis=0, keepdims=True))


def _outproj(x, ma, mb, mc, mod, w_out, g_ffn, w_router):
    b = x.shape[0]
    n = b * SEQ
    nt = SEQ // TOK_TILE
    tok = lambda w: pl.BlockSpec((1, TOK_TILE, w), lambda i, j: (i, j, 0))
    wr = jnp.pad(w_router, ((0, 0), (0, LANES - N_EXPERTS)))
    return pl.pallas_call(
        _outproj_kernel,
        grid=(b, nt),
        in_specs=[
            tok(D_MODEL), tok(A_W), tok(B_W), tok(C_W),
            pl.BlockSpec((1, 6, D_MODEL), lambda i, j: (i, 0, 0)),
            pl.BlockSpec((D_MODEL, D_MODEL), lambda i, j: (0, 0)),
            pl.BlockSpec((1, D_MODEL), lambda i, j: (0, 0)),
            pl.BlockSpec((D_MODEL, LANES), lambda i, j: (0, 0)),
        ],
        out_specs=[
            tok(D_MODEL),
            pl.BlockSpec((TOK_TILE, D_MODEL), lambda i, j: (i * nt + j, 0)),
            pl.BlockSpec((N_EXPERTS, TOK_TILE), lambda i, j: (0, i * nt + j)),
        ],
        out_shape=[
            jax.ShapeDtypeStruct((b, SEQ, D_MODEL), F32),
            jax.ShapeDtypeStruct((n, D_MODEL), BF16),
            jax.ShapeDtypeStruct((N_EXPERTS, n), F32),
        ],
        compiler_params=_cparams("arbitrary", "arbitrary"),
        name="outproj_router",
    )(x, ma, mb, mc, mod, w_out, g_ffn.reshape(1, D_MODEL), wr)


def _route_kernel(aff_ref, pos_ref, rp_ref, *, cap):
    nr = aff_ref.shape[1]
    as_bits = lambda a: lax.bitcast_convert_type(a, I32)

    def bisect(_, carry):
        lo, hi = carry
        mid = lo + ((hi - lo) >> 1)
        ge = jnp.where(as_bits(aff_ref[...]) >= mid, 1.0, 0.0)
        cnt = jnp.sum(jnp.sum(ge, axis=1, keepdims=True), axis=2, keepdims=True)
        keep = cnt >= cap
        return jnp.where(keep, mid, lo), jnp.where(keep, hi, mid)

    lo0 = jnp.zeros((N_EXPERTS, 1, 1), I32)
    hi0 = jnp.full((N_EXPERTS, 1, 1), 0x7F800000, I32)
    thr, _ = lax.fori_loop(0, 31, bisect, (lo0, hi0))

    tri = jnp.where(lax.broadcasted_iota(I32, (LANES, LANES), 0) <= lax.broadcasted_iota(I32, (LANES, LANES), 1),
                    1.0, 0.0).astype(BF16)
    below = jnp.where(lax.broadcasted_iota(I32, (nr, nr), 1) < lax.broadcasted_iota(I32, (nr, nr), 0),
                      1.0, 0.0).astype(BF16)

    def ranks(mask):
        incl = _dot(mask.astype(BF16), tri)
        before = _dot(below, incl.astype(BF16))[:, LANES - 1:]
        return incl - mask + before, before

    for e in range(N_EXPERTS):
        b = as_bits(aff_ref[e])
        t = thr[e]
        gt = b > t
        eq = b == t
        n_gt = jnp.sum(jnp.sum(jnp.where(gt, 1.0, 0.0), axis=0, keepdims=True), axis=1, keepdims=True)
        eq_rank, _ = ranks(jnp.where(eq, 1.0, 0.0))
        sel = jnp.where(gt, 1.0, jnp.where(eq, jnp.where(eq_rank < cap - n_gt, 1.0, 0.0), 0.0))
        pos, before = ranks(sel)
        pos_ref[e] = jnp.where(sel > 0.5, pos, -1.0).astype(I32)
        rp_ref[e] = before


def _route(aff_t, cap):
    n = aff_t.shape[1]
    nr = n // LANES
    pos, rp = pl.pallas_call(
        functools.partial(_route_kernel, cap=cap),
        out_shape=[
            jax.ShapeDtypeStruct((N_EXPERTS, nr, LANES), I32),
            jax.ShapeDtypeStruct((N_EXPERTS, nr, 1), F32),
        ],
        compiler_params=pltpu.CompilerParams(vmem_limit_bytes=VMEM_LIMIT),
        name="route",
    )(aff_t.reshape(N_EXPERTS, nr, LANES))
    per_block = ROUTE_TILE // LANES
    offs = rp[:, ::per_block, 0].astype(I32)
    offs = jnp.concatenate([offs, jnp.full((N_EXPERTS, 1), cap, I32)], axis=1)
    return pos.reshape(N_EXPERTS, n), offs.reshape(-1)


def _align(v):
    return pl.multiple_of(lax.shift_left(lax.shift_right_logical(v, SUBLANE_SHIFT), SUBLANE_SHIFT), SUBLANES)


def _dispatch_kernel(offs_ref, hn_ref, pos_ref, xe_ref, stage, ostage, carry, sem, osem, *, nb, cap):
    t = pl.program_id(0)
    slot = t % 2
    off = lambda e, tt: offs_ref[e * (nb + 1) + tt]
    rows = lax.broadcasted_iota(I32, (WIN, ROUTE_TILE), 0)
    pos = pos_ref[...]
    hn = hn_ref[...]

    def main_copy(e, tt, sl):
        return pltpu.make_async_copy(stage.at[sl, e * WIN:(e + 1) * WIN],
                                     xe_ref.at[e, pl.ds(_align(off(e, tt)), WIN)], sem.at[e])

    @pl.when(t == 0)
    def _():
        carry[...] = jnp.zeros_like(carry)

    onehot = jnp.concatenate(
        [jnp.where(pos[e:e + 1, :] - _align(off(e, t)) == rows, 1.0, 0.0) for e in range(N_EXPERTS)],
        axis=0).astype(BF16)
    stage[slot] = _dot(onehot, hn)

    for e in range(N_EXPERTS):
        start = _align(off(e, t))
        filled = off(e, t + 1) - start
        nwin = lax.shift_right_logical(jnp.maximum(filled - SLACK, 0), CHUNK_SHIFT) + 1
        tail = _align(filled) - jnp.where(nwin > 1, SLACK + (nwin - 1) * CHUNK, 0)
        stage[slot, e * WIN:e * WIN + SUBLANES, :] += carry[e]

        @pl.when(t > 0)
        def _():
            main_copy(e, t - 1, 1 - slot).wait()
        main_copy(e, t, slot).start()

        @pl.when(nwin == 1)
        def _():
            carry[e] = stage[slot, pl.ds(pl.multiple_of(e * WIN + tail, SUBLANES), SUBLANES), :]

        def extra(c, unused):
            lo = pl.multiple_of(start + SLACK + c * CHUNK, SUBLANES)
            chunk_rows = lax.broadcasted_iota(I32, (CHUNK, ROUTE_TILE), 0)
            oh = jnp.where(pos_ref[e:e + 1, :] - lo == chunk_rows, 1.0, 0.0).astype(BF16)
            ostage[...] = _dot(oh, hn_ref[...])
            cp = pltpu.make_async_copy(ostage, xe_ref.at[e, pl.ds(lo, CHUNK)], osem)
            cp.start()
            cp.wait()

            @pl.when(c == nwin - 1)
            def _():
                carry[e] = ostage[pl.ds(pl.multiple_of(tail, SUBLANES), SUBLANES), :]
            return unused

        lax.fori_loop(1, nwin, extra, 0)

    @pl.when(t == nb - 1)
    def _():
        for e in range(N_EXPERTS):
            main_copy(e, t, slot).wait()
        stage[1 - slot, 0:WIN, :] = jnp.zeros((WIN, D_MODEL), F32)
        pads = [pltpu.make_async_copy(stage.at[1 - slot, 0:WIN], xe_ref.at[e, cap:cap + WIN], sem.at[e])
                for e in range(N_EXPERTS)]
        for cp in pads:
            cp.start()
        for cp in pads:
            cp.wait()


def _dispatch(hn, pos, offs, cap):
    n = hn.shape[0]
    nb = n // ROUTE_TILE
    return pl.pallas_call(
        functools.partial(_dispatch_kernel, nb=nb, cap=cap),
        grid_spec=pltpu.PrefetchScalarGridSpec(
            num_scalar_prefetch=1,
            grid=(nb,),
            in_specs=[
                pl.BlockSpec((ROUTE_TILE, D_MODEL), lambda t, o: (t, 0)),
                pl.BlockSpec((N_EXPERTS, ROUTE_TILE), lambda t, o: (0, t)),
            ],
            out_specs=pl.BlockSpec(memory_space=pl.ANY),
            scratch_shapes=[
                pltpu.VMEM((2, N_EXPERTS * WIN, D_MODEL), F32),
                pltpu.VMEM((CHUNK, D_MODEL), F32),
                pltpu.VMEM((N_EXPERTS, SUBLANES, D_MODEL), F32),
                pltpu.SemaphoreType.DMA((N_EXPERTS,)),
                pltpu.SemaphoreType.DMA(()),
            ],
        ),
        out_shape=jax.ShapeDtypeStruct((N_EXPERTS, cap + WIN, D_MODEL), F32),
        compiler_params=_cparams("arbitrary"),
        name="dispatch",
    )(offs, hn, pos)


def _ffn_kernel(xe_ref, wg_ref, wu_ref, wd_ref, ye_ref):
    x = xe_ref[0].astype(BF16)
    g = _dot(x, wg_ref[0])
    u = _dot(x, wu_ref[0])
    hid = (g * jax.nn.sigmoid(g) * u).astype(BF16)
    ye_ref[0] = _dot(hid, wd_ref[0])


def _ffn(xe, w_gate, w_up, w_down, cap):
    til = pl.BlockSpec((1, FFN_TILE, D_MODEL), lambda e, j: (e, j, 0))
    wsp = pl.BlockSpec((1, D_MODEL, D_MODEL), lambda e, j: (e, 0, 0))
    return pl.pallas_call(
        _ffn_kernel,
        grid=(N_EXPERTS, cap // FFN_TILE),
        in_specs=[til, wsp, wsp, wsp],
        out_specs=til,
        out_shape=jax.ShapeDtypeStruct((N_EXPERTS, cap, D_MODEL), F32),
        compiler_params=_cparams("arbitrary", "arbitrary"),
        name="expert_ffn",
    )(xe, w_gate, w_up, w_down)


def _combine_kernel(offs_ref, x_ref, pos_ref, gate_ref, mod_ref, gfin_ref, ye_ref, o_ref,
                    yin, obuf, yacc, sem, osem, *, nb, cap, final):
    t = pl.program_id(0)
    slot = t % 2
    off = lambda e, tt: offs_ref[e * (nb + 1) + tt]
    base = lambda lo: pl.multiple_of(jnp.minimum(_align(lo), cap - WIN), SUBLANES)
    slots = lax.broadcasted_iota(I32, (WIN, ROUTE_TILE), 0)

    def main_copy(e, tt, sl):
        return pltpu.make_async_copy(ye_ref.at[e, pl.ds(base(off(e, tt)), WIN)],
                                     yin.at[sl, e * WIN:(e + 1) * WIN], sem.at[sl, e])

    @pl.when(t == 0)
    def _():
        for e in range(N_EXPERTS):
            main_copy(e, 0, 0).start()

    @pl.when(t + 1 < nb)
    def _():
        for e in range(N_EXPERTS):
            main_copy(e, t + 1, 1 - slot).start()

    pos = pos_ref[...]
    gate = gate_ref[...]

    def weights(e, lo):
        p = pos[e:e + 1, :]
        hit = (p - base(lo) == slots) & (p >= lo) & (p < lo + CHUNK)
        return jnp.where(hit, gate[e:e + 1, :], 0.0).astype(BF16)

    gmat = jnp.concatenate([weights(e, off(e, t)) for e in range(N_EXPERTS)], axis=0)
    for e in range(N_EXPERTS):
        main_copy(e, t, slot).wait()
    yacc[...] = _dot_tn(gmat, yin[slot].astype(BF16))

    for e in range(N_EXPERTS):
        nch = lax.shift_right_logical(off(e, t + 1) - off(e, t) + (CHUNK - 1), CHUNK_SHIFT)

        def extra(c, carry):
            lo = off(e, t) + c * CHUNK
            cp = pltpu.make_async_copy(ye_ref.at[e, pl.ds(base(lo), WIN)], obuf, osem)
            cp.start()
            cp.wait()
            yacc[...] += _dot_tn(weights(e, lo), obuf[...].astype(BF16))
            return carry

        lax.fori_loop(1, nch, extra, 0)

    y = x_ref[...] + mod_ref[0, 5:6, :] * yacc[...]
    if final:
        y = _rms(y) * gfin_ref[...]
    o_ref[...] = y


def _combine(x1, pos, gate, mod, g_final, ye, offs, cap, final):
    n = x1.shape[0]
    nb = n // ROUTE_TILE
    per_seq = SEQ // ROUTE_TILE
    tok = pl.BlockSpec((ROUTE_TILE, D_MODEL), lambda t, o: (t, 0))
    rt = pl.BlockSpec((N_EXPERTS, ROUTE_TILE), lambda t, o: (0, t))
    return pl.pallas_call(
        functools.partial(_combine_kernel, nb=nb, cap=cap, final=final),
        grid_spec=pltpu.PrefetchScalarGridSpec(
            num_scalar_prefetch=1,
            grid=(nb,),
            in_specs=[
                tok, rt, rt,
                pl.BlockSpec((1, 6, D_MODEL), lambda t, o: (t // per_seq, 0, 0)),
                pl.BlockSpec((1, D_MODEL), lambda t, o: (0, 0)),
                pl.BlockSpec(memory_space=pl.ANY),
            ],
            out_specs=tok,
            scratch_shapes=[
                pltpu.VMEM((2, N_EXPERTS * WIN, D_MODEL), F32),
                pltpu.VMEM((WIN, D_MODEL), F32),
                pltpu.VMEM((ROUTE_TILE, D_MODEL), F32),
                pltpu.SemaphoreType.DMA((2, N_EXPERTS)),
                pltpu.SemaphoreType.DMA(()),
            ],
        ),
        out_shape=jax.ShapeDtypeStruct((n, D_MODEL), F32),
        compiler_params=_cparams("arbitrary"),
        name="combine",
    )(offs, x1, pos, gate, mod, g_final.reshape(1, D_MODEL), ye)


def _trunk(x, c, w_ada, b_ada, g_mix, w_in, sink, q_gain, k_gain, rpb, g_grp, w_out,
           g_ffn, w_router, w_gate, w_up, w_down, g_final):
    b = x.shape[0]
    n = b * SEQ
    cap = CAPACITY_FACTOR * n // N_EXPERTS
    depth = w_ada.shape[0]
    for i in range(depth):
        mod = _ada(c, w_ada[i], b_ada[i]).reshape(b, 6, D_MODEL)
        qa, ka, va, qb, kb, vb, qc, kc, vc = _inproj(x, mod, g_mix[i], w_in[i], q_gain[i], k_gain[i])
        ma = _attn_a(qa, ka, va, sink[i], g_grp[i, :A_W])
        mb = _attn_b(qb, kb, vb, g_grp[i, A_W:A_W + B_W])
        mc = _attn_c(qc, kc, vc, rpb[i], g_grp[i, A_W + B_W:])
        x1, hn, aff = _outproj(x, ma, mb, mc, mod, w_out[i], g_ffn[i], w_router[i])
        pos, offs = _route(aff, cap)
        xe = _dispatch(hn, pos, offs, cap)
        ye = _ffn(xe, w_gate[i], w_up[i], w_down[i], cap)
        x = _combine(x1.reshape(n, D_MODEL), pos, aff, mod, g_final, ye, offs, cap,
                     final=(i == depth - 1)).reshape(b, SEQ, D_MODEL)
    return x


def kernel(x_prompt, x_sample, c_prompt, c_sample, w_ada, b_ada, g_mix, w_in, sink, q_gain, k_gain, rpb, g_grp,
           w_out, g_ffn, w_router, w_gate, w_up, w_down, g_final):
    w_in, w_out, w_gate, w_up, w_down = (w.astype(BF16) for w in (w_in, w_out, w_gate, w_up, w_down))
    args = (w_ada, b_ada, g_mix, w_in, sink, q_gain, k_gain, rpb, g_grp, w_out,
            g_ffn, w_router, w_gate, w_up, w_down, g_final)
    return (_trunk(x_prompt, c_prompt, *args), _trunk(x_sample, c_sample, *args))
```

```python
import functools

import numpy as np
import jax
import jax.numpy as jnp
from jax import lax
from jax.experimental import pallas as pl
from jax.experimental.pallas import tpu as pltpu

F32 = jnp.float32
BF16 = jnp.bfloat16
I32 = jnp.int32

D_MODEL = 1024
SEQ = 2048
HEAD_DIM = 64
A_HEADS, A_KV = 8, 2
B_HEADS, B_KV = 4, 2
C_HEADS = 4
A_W, B_W, C_W = A_HEADS * HEAD_DIM, B_HEADS * HEAD_DIM, C_HEADS * HEAD_DIM
IN_WIDTH = 2048
WINDOW = 128
GRID_W = 64
GRID_ROWS = SEQ // GRID_W
NA_ROWS, NA_COLS = 8, 16
N_EXPERTS = 16
CAPACITY_FACTOR = 2
ROPE_THETA = 10000.0
EPS = 1e-6
NEG_INF = -1e30
Q_SCALE = HEAD_DIM ** -0.5

LANES = 128
ROW_ALIGN = 16
ROW_ALIGN_SHIFT = 4
TOK_TILE = 512
Q_TILE_B = 256
NBR_TILE_ROWS = 4
NBR_KEY_ROWS = NA_ROWS + NBR_TILE_ROWS - 1
ROUTE_TILE = 256
WIN = 64
WIN_SHIFT = 6
FFN_TILE = 512
VMEM_LIMIT = 56 * 1024 * 1024


def _cparams(*sem):
    return pltpu.CompilerParams(dimension_semantics=sem, vmem_limit_bytes=VMEM_LIMIT)


def _split_bf16(a):
    hi = a.astype(BF16)
    lo = (a - hi.astype(F32)).astype(BF16)
    return hi, lo


def _dot(a, b):
    return jnp.dot(a, b, preferred_element_type=F32)


def _dot_nt(a, b):
    return lax.dot_general(a, b, (((1,), (1,)), ((), ())), preferred_element_type=F32)


def _dot_tn(a, b):
    return lax.dot_general(a, b, (((0,), (0,)), ((), ())), preferred_element_type=F32)


def _dot3(a, w):
    ah, al = _split_bf16(a)
    wh, wl = _split_bf16(w)
    return _dot(ah, wh) + _dot(al, wh) + _dot(ah, wl)


def _rms(x):
    return x * lax.rsqrt(jnp.mean(x * x, axis=-1, keepdims=True) + EPS)


def _ada_kernel(c_ref, w_ref, b_ref, o_ref):
    c = c_ref[...]
    a = c * jax.nn.sigmoid(c)
    o_ref[...] = _dot3(a, w_ref[...]) + b_ref[...]


def _ada(c, w_ada, b_ada):
    b = c.shape[0]
    return pl.pallas_call(
        _ada_kernel,
        grid=(6,),
        in_specs=[
            pl.BlockSpec((b, D_MODEL), lambda j: (0, 0)),
            pl.BlockSpec((D_MODEL, D_MODEL), lambda j: (0, j)),
            pl.BlockSpec((1, D_MODEL), lambda j: (0, j)),
        ],
        out_specs=pl.BlockSpec((b, D_MODEL), lambda j: (0, j)),
        out_shape=jax.ShapeDtypeStruct((b, 6 * D_MODEL), F32),
        compiler_params=_cparams("arbitrary"),
        name="ada",
    )(c, w_ada, b_ada.reshape(1, 6 * D_MODEL))


def _rotate(x, cos, sin_signed, half):
    w = x.shape[1]
    lane = lax.broadcasted_iota(I32, x.shape, 1)
    first = (lane % (2 * half)) < half
    rot = jnp.where(first, pltpu.roll(x, w - half, 1), pltpu.roll(x, half, 1))
    return x * cos + rot * sin_signed


def _head_rms(x, hm, gain):
    sq = x * x
    sh, sl = _split_bf16(sq)
    ms = (_dot(sh, hm) + _dot(sl, hm)) * (1.0 / HEAD_DIM)
    return x * lax.rsqrt(ms + EPS) * gain


def _inproj_kernel(x_ref, mod_ref, g_ref, w_ref, ca_ref, sa_ref, cb_ref, sb_ref, qg_ref, kg_ref, hm_ref,
                   qa_ref, ka_ref, va_ref, qb_ref, kb_ref, vb_ref, qc_ref, kc_ref, vc_ref):
    x = x_ref[0]
    hn = _rms(x) * g_ref[...]
    hn = hn * (1.0 + mod_ref[0, 1:2, :]) + mod_ref[0, 0:1, :]
    proj = _dot(hn.astype(BF16), w_ref[...])
    ca, sa, cb, sb = ca_ref[...], sa_ref[...], cb_ref[...], sb_ref[...]
    hm = hm_ref[...]

    col = 0
    for j in range(A_W // LANES):
        qa_ref[0, :, j * LANES:(j + 1) * LANES] = (
            _rotate(proj[:, col:col + LANES], ca, sa, HEAD_DIM // 2) * Q_SCALE).astype(BF16)
        col += LANES
    ka_ref[0] = _rotate(proj[:, col:col + LANES], ca, sa, HEAD_DIM // 2).astype(BF16)
    col += LANES
    va_ref[0] = proj[:, col:col + LANES].astype(BF16)
    col += LANES
    for j in range(B_W // LANES):
        q = _head_rms(proj[:, col:col + LANES], hm, qg_ref[...])
        qb_ref[0, :, j * LANES:(j + 1) * LANES] = (_rotate(q, cb, sb, HEAD_DIM // 4) * Q_SCALE).astype(BF16)
        col += LANES
    k = _head_rms(proj[:, col:col + LANES], hm, kg_ref[...])
    kb_ref[0] = _rotate(k, cb, sb, HEAD_DIM // 4).astype(BF16)
    col += LANES
    vb_ref[0] = proj[:, col:col + LANES].astype(BF16)
    col += LANES
    qc_ref[0] = (proj[:, col:col + C_W] * Q_SCALE).astype(BF16)
    col += C_W
    kc_ref[0] = proj[:, col:col + C_W].astype(BF16)
    col += C_W
    vc_ref[0] = proj[:, col:col + C_W].astype(BF16)


def _rope_tables():
    t = np.arange(SEQ, dtype=np.float64)[:, None]
    j = np.arange(LANES)[None, :] % HEAD_DIM
    inv_a = ROPE_THETA ** (-(2.0 * (j % 32)) / HEAD_DIM)
    ang_a = t * inv_a
    sign_a = np.where(j < 32, -1.0, 1.0)
    inv_b = ROPE_THETA ** (-(2.0 * (j % 16)) / (HEAD_DIM // 2))
    pos_b = np.where(j < 32, np.floor(t / GRID_W), t % GRID_W)
    ang_b = pos_b * inv_b
    sign_b = np.where((j % 32) < 16, -1.0, 1.0)
    f = lambda a: jnp.asarray(a, dtype=F32)
    return f(np.cos(ang_a)), f(np.sin(ang_a) * sign_a), f(np.cos(ang_b)), f(np.sin(ang_b) * sign_b)


def _inproj(x, mod, g_mix, w_in, q_gain, k_gain):
    b = x.shape[0]
    ca, sa, cb, sb = _rope_tables()
    hm = jnp.asarray(np.kron(np.eye(LANES // HEAD_DIM), np.ones((HEAD_DIM, HEAD_DIM))), dtype=BF16)
    gain2 = lambda g: jnp.tile(g, LANES // HEAD_DIM).reshape(1, LANES)
    tok = lambda w: pl.BlockSpec((1, TOK_TILE, w), lambda i, j: (i, j, 0))
    tab = pl.BlockSpec((TOK_TILE, LANES), lambda i, j: (j, 0))
    full = lambda r, c: pl.BlockSpec((r, c), lambda i, j: (0, 0))
    widths = (A_W, LANES, LANES, B_W, LANES, LANES, C_W, C_W, C_W)
    return pl.pallas_call(
        _inproj_kernel,
        grid=(b, SEQ // TOK_TILE),
        in_specs=[
            tok(D_MODEL),
            pl.BlockSpec((1, 6, D_MODEL), lambda i, j: (i, 0, 0)),
            full(1, D_MODEL),
            full(D_MODEL, IN_WIDTH),
            tab, tab, tab, tab,
            full(1, LANES), full(1, LANES), full(LANES, LANES),
        ],
        out_specs=[tok(w) for w in widths],
        out_shape=[jax.ShapeDtypeStruct((b, SEQ, w), BF16) for w in widths],
        compiler_params=_cparams("arbitrary", "arbitrary"),
        name="inproj",
    )(x, mod, g_mix.reshape(1, D_MODEL), w_in, ca, sa, cb, sb, gain2(q_gain), gain2(k_gain), hm)


def _softmax_pv(s, v, extra=None):
    m = jnp.max(s, axis=-1, keepdims=True)
    if extra is not None:
        m = jnp.maximum(m, extra)
    p = jnp.exp(s - m)
    l = jnp.sum(p, axis=-1, keepdims=True)
    if extra is not None:
        l = l + jnp.exp(extra - m)
    return _dot(p.astype(BF16), v) * (1.0 / l)


def _attn_a_kernel(sink_ref, q_ref, k_ref, v_ref, g_ref, o_ref):
    span = WINDOW * 3
    group = A_HEADS // A_KV
    rows = group * WINDOW
    ri = lax.broadcasted_iota(I32, (rows, span), 0)
    qi = ri % WINDOW
    ki = lax.broadcasted_iota(I32, (rows, span), 1)
    head_of_row = lax.broadcasted_iota(I32, (rows, 1), 0) // WINDOW
    g = g_ref[...]

    def block(i, carry):
        start = pl.multiple_of(i * WINDOW, WINDOW)
        kstart = pl.multiple_of(jnp.clip(start - WINDOW, 0, SEQ - span), WINDOW)
        q = q_ref[0, pl.ds(start, WINDOW), :]
        k = k_ref[0, pl.ds(kstart, span), :]
        v = v_ref[0, pl.ds(kstart, span), :]
        ok = jnp.abs(ki - qi + (kstart - start)) <= WINDOW
        outs = []
        for kv in range(A_KV):
            heads = range(kv * group, (kv + 1) * group)
            qs = jnp.concatenate([q[:, h * HEAD_DIM:(h + 1) * HEAD_DIM] for h in heads], axis=0)
            sink = jnp.zeros((rows, 1), F32)
            for j, h in enumerate(heads):
                sink = jnp.where(head_of_row == j, sink_ref[h], sink)
            s = _dot_nt(qs, k[:, kv * HEAD_DIM:(kv + 1) * HEAD_DIM])
            s = jnp.where(ok, s, NEG_INF)
            o = _softmax_pv(s, v[:, kv * HEAD_DIM:(kv + 1) * HEAD_DIM], extra=sink)
            outs += [o[j * WINDOW:(j + 1) * WINDOW] for j in range(group)]
        o = jnp.concatenate(outs, axis=1)
        o_ref[0, pl.ds(start, WINDOW), :] = (_rms(o) * g).astype(BF16)
        return carry

    lax.fori_loop(0, SEQ // WINDOW, block, 0)


def _attn_a(q, k, v, sink, g):
    b = q.shape[0]
    seq = lambda w: pl.BlockSpec((1, SEQ, w), lambda i: (i, 0, 0))
    return pl.pallas_call(
        _attn_a_kernel,
        grid=(b,),
        in_specs=[
            pl.BlockSpec(memory_space=pltpu.SMEM),
            seq(A_W), seq(LANES), seq(LANES),
            pl.BlockSpec((1, A_W), lambda i: (0, 0)),
        ],
        out_specs=seq(A_W),
        out_shape=jax.ShapeDtypeStruct((b, SEQ, A_W), BF16),
        compiler_params=_cparams("arbitrary"),
        name="attn_window",
    )(sink, q, k, v, g.reshape(1, A_W))


def _attn_b_kernel(q_ref, k_ref, v_ref, g_ref, o_ref):
    q = q_ref[0]
    k = k_ref[0]
    v = v_ref[0]
    group = B_HEADS // B_KV
    outs = []
    for h in range(B_HEADS):
        kv = h // group
        s = _dot_nt(q[:, h * HEAD_DIM:(h + 1) * HEAD_DIM], k[:, kv * HEAD_DIM:(kv + 1) * HEAD_DIM])
        outs.append(_softmax_pv(s, v[:, kv * HEAD_DIM:(kv + 1) * HEAD_DIM]))
    o = jnp.concatenate(outs, axis=1)
    o_ref[0] = (_rms(o) * g_ref[...]).astype(BF16)


def _attn_b(q, k, v, g):
    b = q.shape[0]
    seq = lambda w: pl.BlockSpec((1, SEQ, w), lambda i, j: (i, 0, 0))
    til = pl.BlockSpec((1, Q_TILE_B, B_W), lambda i, j: (i, j, 0))
    return pl.pallas_call(
        _attn_b_kernel,
        grid=(b, SEQ // Q_TILE_B),
        in_specs=[til, seq(LANES), seq(LANES), pl.BlockSpec((1, B_W), lambda i, j: (0, 0))],
        out_specs=til,
        out_shape=jax.ShapeDtypeStruct((b, SEQ, B_W), BF16),
        compiler_params=_cparams("arbitrary", "arbitrary"),
        name="attn_dense",
    )(q, k, v, g.reshape(1, B_W))


def _attn_c_kernel(q_ref, k_ref, v_ref, bias_ref, g_ref, o_ref):
    nq = NBR_TILE_ROWS * GRID_W
    nkeys = NBR_KEY_ROWS * GRID_W
    ntiles = GRID_ROWS // NBR_TILE_ROWS
    g = g_ref[...]

    def tile(i, carry):
        r0 = i * NBR_TILE_ROWS
        ws = jnp.clip(r0 - NA_ROWS // 2, 0, GRID_ROWS - NBR_KEY_ROWS)
        cls = jnp.where(i == 0, 0, jnp.where(i == ntiles - 1, 2, 1))
        qstart = pl.multiple_of(r0 * GRID_W, GRID_W)
        kstart = pl.multiple_of(ws * GRID_W, GRID_W)
        q = q_ref[0, pl.ds(qstart, nq), :]
        k = k_ref[0, pl.ds(kstart, nkeys), :]
        v = v_ref[0, pl.ds(kstart, nkeys), :]
        outs = []
        for h in range(C_HEADS):
            sl = slice(h * HEAD_DIM, (h + 1) * HEAD_DIM)
            s = _dot_nt(q[:, sl], k[:, sl]) + bias_ref[cls, h]
            outs.append(_softmax_pv(s, v[:, sl]))
        o = jnp.concatenate(outs, axis=1)
        o_ref[0, pl.ds(qstart, nq), :] = (_rms(o) * g).astype(BF16)
        return carry

    lax.fori_loop(0, ntiles, tile, 0)


def _nbr_bias(rpb):
    edge = GRID_W - NA_COLS
    padded = jnp.pad(rpb, ((0, 0), (0, 0), (edge, edge)))
    toe = jnp.stack([padded[..., GRID_W - 1 - c:2 * GRID_W - 1 - c] for c in range(GRID_W)], axis=2)
    c = np.arange(GRID_W)[:, None]
    kc = np.arange(GRID_W)[None, :]
    cs = np.clip(c - NA_COLS // 2, 0, GRID_W - NA_COLS)
    toe = jnp.where((kc >= cs) & (kc < cs + NA_COLS), toe, NEG_INF)
    masked = jnp.full((C_HEADS, GRID_W, GRID_W), NEG_INF, F32)
    ntiles = GRID_ROWS // NBR_TILE_ROWS
    classes = []
    for r0 in (0, NBR_TILE_ROWS, (ntiles - 1) * NBR_TILE_ROWS):
        ws = min(max(r0 - NA_ROWS // 2, 0), GRID_ROWS - NBR_KEY_ROWS)
        per_query_row = []
        for rq in range(NBR_TILE_ROWS):
            r = r0 + rq
            rs = min(max(r - NA_ROWS // 2, 0), GRID_ROWS - NA_ROWS)
            blocks = [toe[:, ws + kr - r + NA_ROWS - 1] if rs <= ws + kr < rs + NA_ROWS else masked
                      for kr in range(NBR_KEY_ROWS)]
            per_query_row.append(jnp.stack(blocks, axis=2))
        classes.append(jnp.stack(per_query_row, axis=1))
    return jnp.stack(classes).reshape(3, C_HEADS, NBR_TILE_ROWS * GRID_W, NBR_KEY_ROWS * GRID_W)


def _attn_c(q, k, v, rpb, g):
    b = q.shape[0]
    seq = pl.BlockSpec((1, SEQ, C_W), lambda i: (i, 0, 0))
    return pl.pallas_call(
        _attn_c_kernel,
        grid=(b,),
        in_specs=[
            seq, seq, seq,
            pl.BlockSpec((3, C_HEADS, NBR_TILE_ROWS * GRID_W, NBR_KEY_ROWS * GRID_W), lambda i: (0, 0, 0, 0)),
            pl.BlockSpec((1, C_W), lambda i: (0, 0)),
        ],
        out_specs=seq,
        out_shape=jax.ShapeDtypeStruct((b, SEQ, C_W), BF16),
        compiler_params=_cparams("arbitrary"),
        name="attn_nbr",
    )(q, k, v, _nbr_bias(rpb), g.reshape(1, C_W))


def _outproj_kernel(x_ref, ma_ref, mb_ref, mc_ref, mod_ref, wo_ref, g_ref, wr_ref, x1_ref, hn_ref, aff_ref):
    attn = (_dot(ma_ref[0], wo_ref[0:A_W, :]) + _dot(mb_ref[0], wo_ref[A_W:A_W + B_W, :])
            + _dot(mc_ref[0], wo_ref[A_W + B_W:, :]))
    x1 = x_ref[0] + mod_ref[0, 2:3, :] * attn
    x1_ref[0] = x1
    hn = _rms(x1) * g_ref[...]
    hn = hn * (1.0 + mod_ref[0, 4:5, :]) + mod_ref[0, 3:4, :]
    hn_ref[...] = hn.astype(BF16)
    logits = _dot3(hn, wr_ref[...])
    lt = logits.T[0:N_EXPERTS, :]
    e = jnp.exp(lt - jnp.max(lt, axis=0, keepdims=True))
    aff_ref[...] = e * (1.0 / jnp.sum(e, axis=0, keepdims=True))


def _outproj(x, ma, mb, mc, mod, w_out, g_ffn, w_router):
    b = x.shape[0]
    n = b * SEQ
    nt = SEQ // TOK_TILE
    tok = lambda w: pl.BlockSpec((1, TOK_TILE, w), lambda i, j: (i, j, 0))
    wr = jnp.pad(w_router, ((0, 0), (0, LANES - N_EXPERTS)))
    return pl.pallas_call(
        _outproj_kernel,
        grid=(b, nt),
        in_specs=[
            tok(D_MODEL), tok(A_W), tok(B_W), tok(C_W),
            pl.BlockSpec((1, 6, D_MODEL), lambda i, j: (i, 0, 0)),
            pl.BlockSpec((D_MODEL, D_MODEL), lambda i, j: (0, 0)),
            pl.BlockSpec((1, D_MODEL), lambda i, j: (0, 0)),
            pl.BlockSpec((D_MODEL, LANES), lambda i, j: (0, 0)),
        ],
        out_specs=[
            tok(D_MODEL),
            pl.BlockSpec((TOK_TILE, D_MODEL), lambda i, j: (i * nt + j, 0)),
            pl.BlockSpec((N_EXPERTS, TOK_TILE), lambda i, j: (0, i * nt + j)),
        ],
        out_shape=[
            jax.ShapeDtypeStruct((b, SEQ, D_MODEL), F32),
            jax.ShapeDtypeStruct((n, D_MODEL), BF16),
            jax.ShapeDtypeStruct((N_EXPERTS, n), F32),
        ],
        compiler_params=_cparams("arbitrary", "arbitrary"),
        name="outproj_router",
    )(x, ma, mb, mc, mod, w_out, g_ffn.reshape(1, D_MODEL), wr)


def _route_kernel(aff_ref, pos_ref, rp_ref, *, cap):
    nr = aff_ref.shape[1]
    as_bits = lambda a: lax.bitcast_convert_type(a, I32)

    def bisect(_, carry):
        lo, hi = carry
        mid = lo + ((hi - lo) >> 1)
        ge = jnp.where(as_bits(aff_ref[...]) >= mid, 1.0, 0.0)
        cnt = jnp.sum(jnp.sum(ge, axis=1, keepdims=True), axis=2, keepdims=True)
        keep = cnt >= cap
        return jnp.where(keep, mid, lo), jnp.where(keep, hi, mid)

    lo0 = jnp.zeros((N_EXPERTS, 1, 1), I32)
    hi0 = jnp.full((N_EXPERTS, 1, 1), 0x7F800000, I32)
    thr, _ = lax.fori_loop(0, 31, bisect, (lo0, hi0))

    tri = jnp.where(lax.broadcasted_iota(I32, (LANES, LANES), 0) <= lax.broadcasted_iota(I32, (LANES, LANES), 1),
                    1.0, 0.0).astype(BF16)
    below = jnp.where(lax.broadcasted_iota(I32, (nr, nr), 1) < lax.broadcasted_iota(I32, (nr, nr), 0),
                      1.0, 0.0).astype(BF16)

    def ranks(mask):
        incl = _dot(mask.astype(BF16), tri)
        before = _dot(below, incl.astype(BF16))[:, LANES - 1:]
        return incl - mask + before, before

    for e in range(N_EXPERTS):
        b = as_bits(aff_ref[e])
        t = thr[e]
        gt = b > t
        eq = b == t
        n_gt = jnp.sum(jnp.sum(jnp.where(gt, 1.0, 0.0), axis=0, keepdims=True), axis=1, keepdims=True)
        eq_rank, _ = ranks(jnp.where(eq, 1.0, 0.0))
        sel = jnp.where(gt, 1.0, jnp.where(eq, jnp.where(eq_rank < cap - n_gt, 1.0, 0.0), 0.0))
        pos, before = ranks(sel)
        pos_ref[e] = jnp.where(sel > 0.5, pos, -1.0).astype(I32)
        rp_ref[e] = before


def _route(aff_t, cap):
    n = aff_t.shape[1]
    nr = n // LANES
    pos, rp = pl.pallas_call(
        functools.partial(_route_kernel, cap=cap),
        out_shape=[
            jax.ShapeDtypeStruct((N_EXPERTS, nr, LANES), I32),
            jax.ShapeDtypeStruct((N_EXPERTS, nr, 1), F32),
        ],
        compiler_params=pltpu.CompilerParams(vmem_limit_bytes=VMEM_LIMIT),
        name="route",
    )(aff_t.reshape(N_EXPERTS, nr, LANES))
    per_block = ROUTE_TILE // LANES
    offs = rp[:, ::per_block, 0].astype(I32)
    offs = jnp.concatenate([offs, jnp.full((N_EXPERTS, 1), cap, I32)], axis=1)
    return pos.reshape(N_EXPERTS, n), offs.reshape(-1)


def _align(v):
    return pl.multiple_of(lax.shift_left(lax.shift_right_logical(v, ROW_ALIGN_SHIFT), ROW_ALIGN_SHIFT), ROW_ALIGN)


def _dispatch_kernel(offs_ref, hn_ref, pos_ref, xe_ref, stage, ostage, carry, sem, osem, *, nb, cap):
    t = pl.program_id(0)
    slot = t % 2
    off = lambda e, tt: offs_ref[e * (nb + 1) + tt]
    rows = lax.broadcasted_iota(I32, (WIN, ROUTE_TILE), 0)
    pos = pos_ref[...]
    hn = hn_ref[...]

    def main_copy(e, tt, sl):
        return pltpu.make_async_copy(stage.at[sl, e * WIN:(e + 1) * WIN],
                                     xe_ref.at[e, pl.ds(_align(off(e, tt)), WIN)], sem.at[e])

    @pl.when(t == 0)
    def _():
        carry[...] = jnp.zeros_like(carry)

    onehot = jnp.concatenate(
        [jnp.where(pos[e:e + 1, :] - _align(off(e, t)) == rows, 1.0, 0.0) for e in range(N_EXPERTS)],
        axis=0).astype(BF16)
    stage[slot] = _dot(onehot, hn).astype(BF16)

    for e in range(N_EXPERTS):
        start = _align(off(e, t))
        filled = off(e, t + 1) - start
        nwin = lax.shift_right_logical(filled, WIN_SHIFT) + 1
        tail = _align(filled) - (nwin - 1) * WIN
        first = stage[slot, e * WIN:e * WIN + ROW_ALIGN, :]
        stage[slot, e * WIN:e * WIN + ROW_ALIGN, :] = first + carry[e]

        @pl.when(t > 0)
        def _():
            main_copy(e, t - 1, 1 - slot).wait()
        main_copy(e, t, slot).start()

        @pl.when(nwin == 1)
        def _():
            carry[e] = stage[slot, pl.ds(pl.multiple_of(e * WIN + tail, ROW_ALIGN), ROW_ALIGN), :]

        def extra(c, unused):
            lo = pl.multiple_of(start + c * WIN, ROW_ALIGN)
            win_rows = lax.broadcasted_iota(I32, (WIN, ROUTE_TILE), 0)
            oh = jnp.where(pos_ref[e:e + 1, :] - lo == win_rows, 1.0, 0.0).astype(BF16)
            ostage[...] = _dot(oh, hn_ref[...]).astype(BF16)
            cp = pltpu.make_async_copy(ostage, xe_ref.at[e, pl.ds(lo, WIN)], osem)
            cp.start()
            cp.wait()

            @pl.when(c == nwin - 1)
            def _():
                carry[e] = ostage[pl.ds(pl.multiple_of(tail, ROW_ALIGN), ROW_ALIGN), :]
            return unused

        lax.fori_loop(1, nwin, extra, 0)

    @pl.when(t == nb - 1)
    def _():
        for e in range(N_EXPERTS):
            main_copy(e, t, slot).wait()
        stage[1 - slot, 0:WIN, :] = jnp.zeros((WIN, D_MODEL), BF16)
        pads = [pltpu.make_async_copy(stage.at[1 - slot, 0:WIN], xe_ref.at[e, cap:cap + WIN], sem.at[e])
                for e in range(N_EXPERTS)]
        for cp in pads:
            cp.start()
        for cp in pads:
            cp.wait()


def _dispatch(hn, pos, offs, cap):
    n = hn.shape[0]
    nb = n // ROUTE_TILE
    return pl.pallas_call(
        functools.partial(_dispatch_kernel, nb=nb, cap=cap),
        grid_spec=pltpu.PrefetchScalarGridSpec(
            num_scalar_prefetch=1,
            grid=(nb,),
            in_specs=[
                pl.BlockSpec((ROUTE_TILE, D_MODEL), lambda t, o: (t, 0)),
                pl.BlockSpec((N_EXPERTS, ROUTE_TILE), lambda t, o: (0, t)),
            ],
            out_specs=pl.BlockSpec(memory_space=pl.ANY),
            scratch_shapes=[
                pltpu.VMEM((2, N_EXPERTS * WIN, D_MODEL), BF16),
                pltpu.VMEM((WIN, D_MODEL), BF16),
                pltpu.VMEM((N_EXPERTS, ROW_ALIGN, D_MODEL), BF16),
                pltpu.SemaphoreType.DMA((N_EXPERTS,)),
                pltpu.SemaphoreType.DMA(()),
            ],
        ),
        out_shape=jax.ShapeDtypeStruct((N_EXPERTS, cap + WIN, D_MODEL), BF16),
        compiler_params=_cparams("arbitrary"),
        name="dispatch",
    )(offs, hn, pos)


def _ffn_kernel(xe_ref, wg_ref, wu_ref, wd_ref, ye_ref):
    x = xe_ref[0]
    g = _dot(x, wg_ref[0])
    u = _dot(x, wu_ref[0])
    hid = (g * jax.nn.sigmoid(g) * u).astype(BF16)
    ye_ref[0] = _dot(hid, wd_ref[0]).astype(BF16)


def _ffn(xe, w_gate, w_up, w_down, cap):
    til = pl.BlockSpec((1, FFN_TILE, D_MODEL), lambda e, j: (e, j, 0))
    wsp = pl.BlockSpec((1, D_MODEL, D_MODEL), lambda e, j: (e, 0, 0))
    return pl.pallas_call(
        _ffn_kernel,
        grid=(N_EXPERTS, cap // FFN_TILE),
        in_specs=[til, wsp, wsp, wsp],
        out_specs=til,
        out_shape=jax.ShapeDtypeStruct((N_EXPERTS, cap, D_MODEL), BF16),
        compiler_params=_cparams("arbitrary", "arbitrary"),
        name="expert_ffn",
    )(xe, w_gate, w_up, w_down)


def _combine_kernel(offs_ref, x_ref, pos_ref, gate_ref, mod_ref, gfin_ref, ye_ref, o_ref,
                    yin, obuf, yacc, sem, osem, *, nb, cap, final):
    t = pl.program_id(0)
    slot = t % 2
    off = lambda e, tt: offs_ref[e * (nb + 1) + tt]
    base = lambda lo: pl.multiple_of(jnp.minimum(lo, cap - WIN), ROW_ALIGN)
    slots = lax.broadcasted_iota(I32, (WIN, ROUTE_TILE), 0)

    def main_copy(e, tt, sl):
        return pltpu.make_async_copy(ye_ref.at[e, pl.ds(base(_align(off(e, tt))), WIN)],
                                     yin.at[sl, e * WIN:(e + 1) * WIN], sem.at[sl, e])

    @pl.when(t == 0)
    def _():
        for e in range(N_EXPERTS):
            main_copy(e, 0, 0).start()

    @pl.when(t + 1 < nb)
    def _():
        for e in range(N_EXPERTS):
            main_copy(e, t + 1, 1 - slot).start()

    pos = pos_ref[...]
    gate = gate_ref[...]

    def weights(e, lo):
        p = pos[e:e + 1, :]
        hit = (p - base(lo) == slots) & (p >= lo) & (p < lo + WIN)
        return jnp.where(hit, gate[e:e + 1, :], 0.0).astype(BF16)

    gmat = jnp.concatenate([weights(e, _align(off(e, t))) for e in range(N_EXPERTS)], axis=0)
    for e in range(N_EXPERTS):
        main_copy(e, t, slot).wait()
    yacc[...] = _dot_tn(gmat, yin[slot])

    for e in range(N_EXPERTS):
        start = _align(off(e, t))
        nwin = lax.shift_right_logical(jnp.maximum(off(e, t + 1) - start - 1, 0), WIN_SHIFT) + 1

        def extra(c, carry):
            lo = start + c * WIN
            cp = pltpu.make_async_copy(ye_ref.at[e, pl.ds(base(lo), WIN)], obuf, osem)
            cp.start()
            cp.wait()
            yacc[...] += _dot_tn(weights(e, lo), obuf[...])
            return carry

        lax.fori_loop(1, nwin, extra, 0)

    y = x_ref[...] + mod_ref[0, 5:6, :] * yacc[...]
    if final:
        y = _rms(y) * gfin_ref[...]
    o_ref[...] = y


def _combine(x1, pos, gate, mod, g_final, ye, offs, cap, final):
    n = x1.shape[0]
    nb = n // ROUTE_TILE
    per_seq = SEQ // ROUTE_TILE
    tok = pl.BlockSpec((ROUTE_TILE, D_MODEL), lambda t, o: (t, 0))
    rt = pl.BlockSpec((N_EXPERTS, ROUTE_TILE), lambda t, o: (0, t))
    return pl.pallas_call(
        functools.partial(_combine_kernel, nb=nb, cap=cap, final=final),
        grid_spec=pltpu.PrefetchScalarGridSpec(
            num_scalar_prefetch=1,
            grid=(nb,),
            in_specs=[
                tok, rt, rt,
                pl.BlockSpec((1, 6, D_MODEL), lambda t, o: (t // per_seq, 0, 0)),
                pl.BlockSpec((1, D_MODEL), lambda t, o: (0, 0)),
                pl.BlockSpec(memory_space=pl.ANY),
            ],
            out_specs=tok,
            scratch_shapes=[
                pltpu.VMEM((2, N_EXPERTS * WIN, D_MODEL), BF16),
                pltpu.VMEM((WIN, D_MODEL), BF16),
                pltpu.VMEM((ROUTE_TILE, D_MODEL), F32),
                pltpu.SemaphoreType.DMA((2, N_EXPERTS)),
                pltpu.SemaphoreType.DMA(()),
            ],
        ),
        out_shape=jax.ShapeDtypeStruct((n, D_MODEL), F32),
        compiler_params=_cparams("arbitrary"),
        name="combine",
    )(offs, x1, pos, gate, mod, g_final.reshape(1, D_MODEL), ye)


def _trunk(x, c, w_ada, b_ada, g_mix, w_in, sink, q_gain, k_gain, rpb, g_grp, w_out,
           g_ffn, w_router, w_gate, w_up, w_down, g_final):
    b = x.shape[0]
    n = b * SEQ
    cap = CAPACITY_FACTOR * n // N_EXPERTS
    depth = w_ada.shape[0]
    for i in range(depth):
        mod = _ada(c, w_ada[i], b_ada[i]).reshape(b, 6, D_MODEL)
        qa, ka, va, qb, kb, vb, qc, kc, vc = _inproj(x, mod, g_mix[i], w_in[i], q_gain[i], k_gain[i])
        ma = _attn_a(qa, ka, va, sink[i], g_grp[i, :A_W])
        mb = _attn_b(qb, kb, vb, g_grp[i, A_W:A_W + B_W])
        mc = _attn_c(qc, kc, vc, rpb[i], g_grp[i, A_W + B_W:])
        x1, hn, aff = _outproj(x, ma, mb, mc, mod, w_out[i], g_ffn[i], w_router[i])
        pos, offs = _route(aff, cap)
        xe = _dispatch(hn, pos, offs, cap)
        ye = _ffn(xe, w_gate[i], w_up[i], w_down[i], cap)
        x = _combine(x1.reshape(n, D_MODEL), pos, aff, mod, g_final, ye, offs, cap,
                     final=(i == depth - 1)).reshape(b, SEQ, D_MODEL)
    return x


def kernel(x_prompt, x_sample, c_prompt, c_sample, w_ada, b_ada, g_mix, w_in, sink, q_gain, k_gain, rpb, g_grp,
           w_out, g_ffn, w_router, w_gate, w_up, w_down, g_final):
    w_in, w_out, w_gate, w_up, w_down = (w.astype(BF16) for w in (w_in, w_out, w_gate, w_up, w_down))
    args = (w_ada, b_ada, g_mix, w_in, sink, q_gain, k_gain, rpb, g_grp, w_out,
            g_ffn, w_router, w_gate, w_up, w_down, g_final)
    return (_trunk(x_prompt, c_prompt, *args), _trunk(x_sample, c_sample, *args))
```

```python
import functools

import numpy as np
import jax
import jax.numpy as jnp
from jax import lax
from jax.experimental import pallas as pl
from jax.experimental.pallas import tpu as pltpu

F32 = jnp.float32
BF16 = jnp.bfloat16
I32 = jnp.int32

D_MODEL = 1024
SEQ = 2048
HEAD_DIM = 64
A_HEADS, A_KV = 8, 2
B_HEADS, B_KV = 4, 2
C_HEADS = 4
A_W, B_W, C_W = A_HEADS * HEAD_DIM, B_HEADS * HEAD_DIM, C_HEADS * HEAD_DIM
IN_WIDTH = 2048
WINDOW = 128
GRID_W = 64
GRID_ROWS = SEQ // GRID_W
NA_ROWS, NA_COLS = 8, 16
N_EXPERTS = 16
CAPACITY_FACTOR = 2
ROPE_THETA = 10000.0
EPS = 1e-6
NEG_INF = -1e30
Q_SCALE = HEAD_DIM ** -0.5

LANES = 128
ROW_ALIGN = 16
ROW_ALIGN_SHIFT = 4
TOK_TILE = 512
Q_TILE_B = 256
NBR_TILE_ROWS = 4
NBR_KEY_ROWS = NA_ROWS + NBR_TILE_ROWS - 1
ROUTE_TILE = 256
ROUTE_SEQS = 16
WIN = 64
WIN_SHIFT = 6
FFN_TILE = 512
VMEM_LIMIT = 56 * 1024 * 1024


def _cparams(*sem):
    return pltpu.CompilerParams(dimension_semantics=sem, vmem_limit_bytes=VMEM_LIMIT)


def _split_bf16(a):
    hi = a.astype(BF16)
    lo = (a - hi.astype(F32)).astype(BF16)
    return hi, lo


def _dot(a, b):
    return jnp.dot(a, b, preferred_element_type=F32)


def _dot_nt(a, b):
    return lax.dot_general(a, b, (((1,), (1,)), ((), ())), preferred_element_type=F32)


def _dot_tn(a, b):
    return lax.dot_general(a, b, (((0,), (0,)), ((), ())), preferred_element_type=F32)


def _dot3(a, w):
    ah, al = _split_bf16(a)
    wh, wl = _split_bf16(w)
    return _dot(ah, wh) + _dot(al, wh) + _dot(ah, wl)


def _rms(x):
    return x * lax.rsqrt(jnp.mean(x * x, axis=-1, keepdims=True) + EPS)


def _ada_kernel(c_ref, w_ref, b_ref, o_ref):
    c = c_ref[...]
    a = c * jax.nn.sigmoid(c)
    o_ref[...] = _dot3(a, w_ref[...]) + b_ref[...]


def _ada(c, w_ada, b_ada):
    b = c.shape[0]
    return pl.pallas_call(
        _ada_kernel,
        grid=(6,),
        in_specs=[
            pl.BlockSpec((b, D_MODEL), lambda j: (0, 0)),
            pl.BlockSpec((D_MODEL, D_MODEL), lambda j: (0, j)),
            pl.BlockSpec((1, D_MODEL), lambda j: (0, j)),
        ],
        out_specs=pl.BlockSpec((b, D_MODEL), lambda j: (0, j)),
        out_shape=jax.ShapeDtypeStruct((b, 6 * D_MODEL), F32),
        compiler_params=_cparams("arbitrary"),
        name="ada",
    )(c, w_ada, b_ada.reshape(1, 6 * D_MODEL))


def _rotate(x, cos, sin_signed, half):
    w = x.shape[1]
    lane = lax.broadcasted_iota(I32, x.shape, 1)
    first = (lane % (2 * half)) < half
    rot = jnp.where(first, pltpu.roll(x, w - half, 1), pltpu.roll(x, half, 1))
    return x * cos + rot * sin_signed


def _head_rms(x, hm, gain):
    sq = x * x
    sh, sl = _split_bf16(sq)
    ms = (_dot(sh, hm) + _dot(sl, hm)) * (1.0 / HEAD_DIM)
    return x * lax.rsqrt(ms + EPS) * gain


def _inproj_kernel(x_ref, mod_ref, g_ref, w_ref, ca_ref, sa_ref, cb_ref, sb_ref, qg_ref, kg_ref, hm_ref,
                   qa_ref, ka_ref, va_ref, qb_ref, kb_ref, vb_ref, qc_ref, kc_ref, vc_ref):
    x = x_ref[0]
    hn = _rms(x) * g_ref[...]
    hn = hn * (1.0 + mod_ref[0, 1:2, :]) + mod_ref[0, 0:1, :]
    proj = _dot(hn.astype(BF16), w_ref[...])
    ca, sa, cb, sb = ca_ref[...], sa_ref[...], cb_ref[...], sb_ref[...]
    hm = hm_ref[...]

    col = 0
    for j in range(A_W // LANES):
        qa_ref[0, :, j * LANES:(j + 1) * LANES] = (
            _rotate(proj[:, col:col + LANES], ca, sa, HEAD_DIM // 2) * Q_SCALE).astype(BF16)
        col += LANES
    ka_ref[0] = _rotate(proj[:, col:col + LANES], ca, sa, HEAD_DIM // 2).astype(BF16)
    col += LANES
    va_ref[0] = proj[:, col:col + LANES].astype(BF16)
    col += LANES
    for j in range(B_W // LANES):
        q = _head_rms(proj[:, col:col + LANES], hm, qg_ref[...])
        qb_ref[0, :, j * LANES:(j + 1) * LANES] = (_rotate(q, cb, sb, HEAD_DIM // 4) * Q_SCALE).astype(BF16)
        col += LANES
    k = _head_rms(proj[:, col:col + LANES], hm, kg_ref[...])
    kb_ref[0] = _rotate(k, cb, sb, HEAD_DIM // 4).astype(BF16)
    col += LANES
    vb_ref[0] = proj[:, col:col + LANES].astype(BF16)
    col += LANES
    qc_ref[0] = (proj[:, col:col + C_W] * Q_SCALE).astype(BF16)
    col += C_W
    kc_ref[0] = proj[:, col:col + C_W].astype(BF16)
    col += C_W
    vc_ref[0] = proj[:, col:col + C_W].astype(BF16)


def _rope_tables():
    t = np.arange(SEQ, dtype=np.float64)[:, None]
    j = np.arange(LANES)[None, :] % HEAD_DIM
    inv_a = ROPE_THETA ** (-(2.0 * (j % 32)) / HEAD_DIM)
    ang_a = t * inv_a
    sign_a = np.where(j < 32, -1.0, 1.0)
    inv_b = ROPE_THETA ** (-(2.0 * (j % 16)) / (HEAD_DIM // 2))
    pos_b = np.where(j < 32, np.floor(t / GRID_W), t % GRID_W)
    ang_b = pos_b * inv_b
    sign_b = np.where((j % 32) < 16, -1.0, 1.0)
    f = lambda a: jnp.asarray(a, dtype=F32)
    return f(np.cos(ang_a)), f(np.sin(ang_a) * sign_a), f(np.cos(ang_b)), f(np.sin(ang_b) * sign_b)


def _inproj(x, mod, g_mix, w_in, q_gain, k_gain):
    b = x.shape[0]
    ca, sa, cb, sb = _rope_tables()
    hm = jnp.asarray(np.kron(np.eye(LANES // HEAD_DIM), np.ones((HEAD_DIM, HEAD_DIM))), dtype=BF16)
    gain2 = lambda g: jnp.tile(g, LANES // HEAD_DIM).reshape(1, LANES)
    tok = lambda w: pl.BlockSpec((1, TOK_TILE, w), lambda i, j: (i, j, 0))
    tab = pl.BlockSpec((TOK_TILE, LANES), lambda i, j: (j, 0))
    full = lambda r, c: pl.BlockSpec((r, c), lambda i, j: (0, 0))
    widths = (A_W, LANES, LANES, B_W, LANES, LANES, C_W, C_W, C_W)
    return pl.pallas_call(
        _inproj_kernel,
        grid=(b, SEQ // TOK_TILE),
        in_specs=[
            tok(D_MODEL),
            pl.BlockSpec((1, 6, D_MODEL), lambda i, j: (i, 0, 0)),
            full(1, D_MODEL),
            full(D_MODEL, IN_WIDTH),
            tab, tab, tab, tab,
            full(1, LANES), full(1, LANES), full(LANES, LANES),
        ],
        out_specs=[tok(w) for w in widths],
        out_shape=[jax.ShapeDtypeStruct((b, SEQ, w), BF16) for w in widths],
        compiler_params=_cparams("arbitrary", "arbitrary"),
        name="inproj",
    )(x, mod, g_mix.reshape(1, D_MODEL), w_in, ca, sa, cb, sb, gain2(q_gain), gain2(k_gain), hm)


def _softmax_pv(s, v, extra=None):
    m = jnp.max(s, axis=-1, keepdims=True)
    if extra is not None:
        m = jnp.maximum(m, extra)
    p = jnp.exp(s - m)
    l = jnp.sum(p, axis=-1, keepdims=True)
    if extra is not None:
        l = l + jnp.exp(extra - m)
    return _dot(p.astype(BF16), v) * (1.0 / l)


def _attn_a_kernel(sink_ref, q_ref, k_ref, v_ref, g_ref, o_ref):
    span = WINDOW * 3
    group = A_HEADS // A_KV
    rows = group * WINDOW
    ri = lax.broadcasted_iota(I32, (rows, span), 0)
    qi = ri % WINDOW
    ki = lax.broadcasted_iota(I32, (rows, span), 1)
    head_of_row = lax.broadcasted_iota(I32, (rows, 1), 0) // WINDOW
    g = g_ref[...]

    def block(i, carry):
        start = pl.multiple_of(i * WINDOW, WINDOW)
        kstart = pl.multiple_of(jnp.clip(start - WINDOW, 0, SEQ - span), WINDOW)
        q = q_ref[0, pl.ds(start, WINDOW), :]
        k = k_ref[0, pl.ds(kstart, span), :]
        v = v_ref[0, pl.ds(kstart, span), :]
        ok = jnp.abs(ki - qi + (kstart - start)) <= WINDOW
        outs = []
        for kv in range(A_KV):
            heads = range(kv * group, (kv + 1) * group)
            qs = jnp.concatenate([q[:, h * HEAD_DIM:(h + 1) * HEAD_DIM] for h in heads], axis=0)
            sink = jnp.zeros((rows, 1), F32)
            for j, h in enumerate(heads):
                sink = jnp.where(head_of_row == j, sink_ref[h], sink)
            s = _dot_nt(qs, k[:, kv * HEAD_DIM:(kv + 1) * HEAD_DIM])
            s = jnp.where(ok, s, NEG_INF)
            o = _softmax_pv(s, v[:, kv * HEAD_DIM:(kv + 1) * HEAD_DIM], extra=sink)
            outs += [o[j * WINDOW:(j + 1) * WINDOW] for j in range(group)]
        o = jnp.concatenate(outs, axis=1)
        o_ref[0, pl.ds(start, WINDOW), :] = (_rms(o) * g).astype(BF16)
        return carry

    lax.fori_loop(0, SEQ // WINDOW, block, 0)


def _attn_a(q, k, v, sink, g):
    b = q.shape[0]
    seq = lambda w: pl.BlockSpec((1, SEQ, w), lambda i: (i, 0, 0))
    return pl.pallas_call(
        _attn_a_kernel,
        grid=(b,),
        in_specs=[
            pl.BlockSpec(memory_space=pltpu.SMEM),
            seq(A_W), seq(LANES), seq(LANES),
            pl.BlockSpec((1, A_W), lambda i: (0, 0)),
        ],
        out_specs=seq(A_W),
        out_shape=jax.ShapeDtypeStruct((b, SEQ, A_W), BF16),
        compiler_params=_cparams("arbitrary"),
        name="attn_window",
    )(sink, q, k, v, g.reshape(1, A_W))


def _attn_b_kernel(q_ref, k_ref, v_ref, g_ref, o_ref):
    q = q_ref[0]
    k = k_ref[0]
    v = v_ref[0]
    group = B_HEADS // B_KV
    outs = []
    for h in range(B_HEADS):
        kv = h // group
        s = _dot_nt(q[:, h * HEAD_DIM:(h + 1) * HEAD_DIM], k[:, kv * HEAD_DIM:(kv + 1) * HEAD_DIM])
        outs.append(_softmax_pv(s, v[:, kv * HEAD_DIM:(kv + 1) * HEAD_DIM]))
    o = jnp.concatenate(outs, axis=1)
    o_ref[0] = (_rms(o) * g_ref[...]).astype(BF16)


def _attn_b(q, k, v, g):
    b = q.shape[0]
    seq = lambda w: pl.BlockSpec((1, SEQ, w), lambda i, j: (i, 0, 0))
    til = pl.BlockSpec((1, Q_TILE_B, B_W), lambda i, j: (i, j, 0))
    return pl.pallas_call(
        _attn_b_kernel,
        grid=(b, SEQ // Q_TILE_B),
        in_specs=[til, seq(LANES), seq(LANES), pl.BlockSpec((1, B_W), lambda i, j: (0, 0))],
        out_specs=til,
        out_shape=jax.ShapeDtypeStruct((b, SEQ, B_W), BF16),
        compiler_params=_cparams("arbitrary", "arbitrary"),
        name="attn_dense",
    )(q, k, v, g.reshape(1, B_W))


def _attn_c_kernel(q_ref, k_ref, v_ref, bias_ref, g_ref, o_ref):
    nq = NBR_TILE_ROWS * GRID_W
    nkeys = NBR_KEY_ROWS * GRID_W
    ntiles = GRID_ROWS // NBR_TILE_ROWS
    g = g_ref[...]

    def tile(i, carry):
        r0 = i * NBR_TILE_ROWS
        ws = jnp.clip(r0 - NA_ROWS // 2, 0, GRID_ROWS - NBR_KEY_ROWS)
        cls = jnp.where(i == 0, 0, jnp.where(i == ntiles - 1, 2, 1))
        qstart = pl.multiple_of(r0 * GRID_W, GRID_W)
        kstart = pl.multiple_of(ws * GRID_W, GRID_W)
        q = q_ref[0, pl.ds(qstart, nq), :]
        k = k_ref[0, pl.ds(kstart, nkeys), :]
        v = v_ref[0, pl.ds(kstart, nkeys), :]
        outs = []
        for h in range(C_HEADS):
            sl = slice(h * HEAD_DIM, (h + 1) * HEAD_DIM)
            s = _dot_nt(q[:, sl], k[:, sl]) + bias_ref[cls, h]
            outs.append(_softmax_pv(s, v[:, sl]))
        o = jnp.concatenate(outs, axis=1)
        o_ref[0, pl.ds(qstart, nq), :] = (_rms(o) * g).astype(BF16)
        return carry

    lax.fori_loop(0, ntiles, tile, 0)


def _nbr_bias(rpb):
    edge = GRID_W - NA_COLS
    padded = jnp.pad(rpb, ((0, 0), (0, 0), (edge, edge)))
    toe = jnp.stack([padded[..., GRID_W - 1 - c:2 * GRID_W - 1 - c] for c in range(GRID_W)], axis=2)
    c = np.arange(GRID_W)[:, None]
    kc = np.arange(GRID_W)[None, :]
    cs = np.clip(c - NA_COLS // 2, 0, GRID_W - NA_COLS)
    toe = jnp.where((kc >= cs) & (kc < cs + NA_COLS), toe, NEG_INF)
    masked = jnp.full((C_HEADS, GRID_W, GRID_W), NEG_INF, F32)
    ntiles = GRID_ROWS // NBR_TILE_ROWS
    classes = []
    for r0 in (0, NBR_TILE_ROWS, (ntiles - 1) * NBR_TILE_ROWS):
        ws = min(max(r0 - NA_ROWS // 2, 0), GRID_ROWS - NBR_KEY_ROWS)
        per_query_row = []
        for rq in range(NBR_TILE_ROWS):
            r = r0 + rq
            rs = min(max(r - NA_ROWS // 2, 0), GRID_ROWS - NA_ROWS)
            blocks = [toe[:, ws + kr - r + NA_ROWS - 1] if rs <= ws + kr < rs + NA_ROWS else masked
                      for kr in range(NBR_KEY_ROWS)]
            per_query_row.append(jnp.stack(blocks, axis=2))
        classes.append(jnp.stack(per_query_row, axis=1))
    return jnp.stack(classes).reshape(3, C_HEADS, NBR_TILE_ROWS * GRID_W, NBR_KEY_ROWS * GRID_W)


def _attn_c(q, k, v, rpb, g):
    b = q.shape[0]
    seq = pl.BlockSpec((1, SEQ, C_W), lambda i: (i, 0, 0))
    return pl.pallas_call(
        _attn_c_kernel,
        grid=(b,),
        in_specs=[
            seq, seq, seq,
            pl.BlockSpec((3, C_HEADS, NBR_TILE_ROWS * GRID_W, NBR_KEY_ROWS * GRID_W), lambda i: (0, 0, 0, 0)),
            pl.BlockSpec((1, C_W), lambda i: (0, 0)),
        ],
        out_specs=seq,
        out_shape=jax.ShapeDtypeStruct((b, SEQ, C_W), BF16),
        compiler_params=_cparams("arbitrary"),
        name="attn_nbr",
    )(q, k, v, _nbr_bias(rpb), g.reshape(1, C_W))


def _outproj_kernel(x_ref, ma_ref, mb_ref, mc_ref, mod_ref, wo_ref, g_ref, wr_ref, x1_ref, hn_ref, aff_ref):
    attn = (_dot(ma_ref[0], wo_ref[0:A_W, :]) + _dot(mb_ref[0], wo_ref[A_W:A_W + B_W, :])
            + _dot(mc_ref[0], wo_ref[A_W + B_W:, :]))
    x1 = x_ref[0] + mod_ref[0, 2:3, :] * attn
    x1_ref[0] = x1
    hn = _rms(x1) * g_ref[...]
    hn = hn * (1.0 + mod_ref[0, 4:5, :]) + mod_ref[0, 3:4, :]
    hn_ref[...] = hn.astype(BF16)
    logits = _dot3(hn, wr_ref[...])
    lt = logits.T[0:N_EXPERTS, :]
    e = jnp.exp(lt - jnp.max(lt, axis=0, keepdims=True))
    aff_ref[...] = e * (1.0 / jnp.sum(e, axis=0, keepdims=True))


def _outproj(x, ma, mb, mc, mod, w_out, g_ffn, w_router):
    b = x.shape[0]
    n = b * SEQ
    nt = SEQ // TOK_TILE
    tok = lambda w: pl.BlockSpec((1, TOK_TILE, w), lambda i, j: (i, j, 0))
    wr = jnp.pad(w_router, ((0, 0), (0, LANES - N_EXPERTS)))
    return pl.pallas_call(
        _outproj_kernel,
        grid=(b, nt),
        in_specs=[
            tok(D_MODEL), tok(A_W), tok(B_W), tok(C_W),
            pl.BlockSpec((1, 6, D_MODEL), lambda i, j: (i, 0, 0)),
            pl.BlockSpec((D_MODEL, D_MODEL), lambda i, j: (0, 0)),
            pl.BlockSpec((1, D_MODEL), lambda i, j: (0, 0)),
            pl.BlockSpec((D_MODEL, LANES), lambda i, j: (0, 0)),
        ],
        out_specs=[
            tok(D_MODEL),
            pl.BlockSpec((TOK_TILE, D_MODEL), lambda i, j: (i * nt + j, 0)),
            pl.BlockSpec((N_EXPERTS, TOK_TILE), lambda i, j: (0, i * nt + j)),
        ],
        out_shape=[
            jax.ShapeDtypeStruct((b, SEQ, D_MODEL), F32),
            jax.ShapeDtypeStruct((n, D_MODEL), BF16),
            jax.ShapeDtypeStruct((N_EXPERTS, n), F32),
        ],
        compiler_params=_cparams("arbitrary", "arbitrary"),
        name="outproj_router",
    )(x, ma, mb, mc, mod, w_out, g_ffn.reshape(1, D_MODEL), wr)


def _route_kernel(aff_ref, pos_ref, rp_ref, *, cap):
    nr = aff_ref.shape[1]
    as_bits = lambda a: lax.bitcast_convert_type(a, I32)

    def bisect(_, carry):
        lo, hi = carry
        mid = lo + ((hi - lo) >> 1)
        ge = jnp.where(as_bits(aff_ref[...]) >= mid, 1.0, 0.0)
        cnt = jnp.sum(jnp.sum(ge, axis=1, keepdims=True), axis=2, keepdims=True)
        keep = cnt >= cap
        return jnp.where(keep, mid, lo), jnp.where(keep, hi, mid)

    lo0 = jnp.zeros((N_EXPERTS, 1, 1), I32)
    hi0 = jnp.full((N_EXPERTS, 1, 1), 0x7F800000, I32)
    thr, _ = lax.fori_loop(0, 31, bisect, (lo0, hi0))

    tri = jnp.where(lax.broadcasted_iota(I32, (LANES, LANES), 0) <= lax.broadcasted_iota(I32, (LANES, LANES), 1),
                    1.0, 0.0).astype(BF16)
    below = jnp.where(lax.broadcasted_iota(I32, (nr, nr), 1) < lax.broadcasted_iota(I32, (nr, nr), 0),
                      1.0, 0.0).astype(BF16)

    def ranks(mask):
        incl = _dot(mask.astype(BF16), tri)
        before = _dot(below, incl.astype(BF16))[:, LANES - 1:]
        return incl - mask + before, before

    for e in range(N_EXPERTS):
        b = as_bits(aff_ref[e])
        t = thr[e]
        gt = b > t
        eq = b == t
        n_gt = jnp.sum(jnp.sum(jnp.where(gt, 1.0, 0.0), axis=0, keepdims=True), axis=1, keepdims=True)
        eq_rank, _ = ranks(jnp.where(eq, 1.0, 0.0))
        sel = jnp.where(gt, 1.0, jnp.where(eq, jnp.where(eq_rank < cap - n_gt, 1.0, 0.0), 0.0))
        pos, before = ranks(sel)
        pos_ref[e] = jnp.where(sel > 0.5, pos, -1.0).astype(I32)
        rp_ref[e] = before


def _block_geometry(b):
    sb = min(b, ROUTE_SEQS)
    return sb, ROUTE_TILE // sb, b // sb


def _route(aff_t, cap, b):
    n = aff_t.shape[1]
    nr = n // LANES
    sb, tt, groups = _block_geometry(b)
    aff_t = aff_t.reshape(N_EXPERTS, groups, sb, SEQ // tt, tt).transpose(0, 3, 1, 2, 4).reshape(N_EXPERTS, n)
    pos, rp = pl.pallas_call(
        functools.partial(_route_kernel, cap=cap),
        out_shape=[
            jax.ShapeDtypeStruct((N_EXPERTS, nr, LANES), I32),
            jax.ShapeDtypeStruct((N_EXPERTS, nr, 1), F32),
        ],
        compiler_params=pltpu.CompilerParams(vmem_limit_bytes=VMEM_LIMIT),
        name="route",
    )(aff_t.reshape(N_EXPERTS, nr, LANES))
    per_block = ROUTE_TILE // LANES
    offs = rp[:, ::per_block, 0].astype(I32)
    offs = jnp.concatenate([offs, jnp.full((N_EXPERTS, 1), cap, I32)], axis=1)
    return pos.reshape(N_EXPERTS, n), aff_t, offs.reshape(-1)


def _align(v):
    return pl.multiple_of(lax.shift_left(lax.shift_right_logical(v, ROW_ALIGN_SHIFT), ROW_ALIGN_SHIFT), ROW_ALIGN)


def _dispatch_kernel(offs_ref, hn_ref, pos_ref, xe_ref, stage, ostage, carry, sem, osem, *, nb, cap):
    t = pl.program_id(0)
    slot = t % 2
    off = lambda e, tt: offs_ref[e * (nb + 1) + tt]
    rows = lax.broadcasted_iota(I32, (WIN, ROUTE_TILE), 0)
    pos = pos_ref[...]
    hn = hn_ref[...].reshape(ROUTE_TILE, D_MODEL)

    def main_copy(e, tt, sl):
        return pltpu.make_async_copy(stage.at[sl, e * WIN:(e + 1) * WIN],
                                     xe_ref.at[e, pl.ds(_align(off(e, tt)), WIN)], sem.at[e])

    @pl.when(t == 0)
    def _():
        carry[...] = jnp.zeros_like(carry)

    onehot = jnp.concatenate(
        [jnp.where(pos[e:e + 1, :] - _align(off(e, t)) == rows, 1.0, 0.0) for e in range(N_EXPERTS)],
        axis=0).astype(BF16)
    stage[slot] = _dot(onehot, hn).astype(BF16)

    for e in range(N_EXPERTS):
        start = _align(off(e, t))
        filled = off(e, t + 1) - start
        nwin = lax.shift_right_logical(filled, WIN_SHIFT) + 1
        tail = _align(filled) - (nwin - 1) * WIN
        first = stage[slot, e * WIN:e * WIN + ROW_ALIGN, :]
        stage[slot, e * WIN:e * WIN + ROW_ALIGN, :] = first + carry[e]

        @pl.when(t > 0)
        def _():
            main_copy(e, t - 1, 1 - slot).wait()
        main_copy(e, t, slot).start()

        @pl.when(nwin == 1)
        def _():
            carry[e] = stage[slot, pl.ds(pl.multiple_of(e * WIN + tail, ROW_ALIGN), ROW_ALIGN), :]

        def extra(c, unused):
            lo = pl.multiple_of(start + c * WIN, ROW_ALIGN)
            win_rows = lax.broadcasted_iota(I32, (WIN, ROUTE_TILE), 0)
            oh = jnp.where(pos_ref[e:e + 1, :] - lo == win_rows, 1.0, 0.0).astype(BF16)
            ostage[...] = _dot(oh, hn_ref[...].reshape(ROUTE_TILE, D_MODEL)).astype(BF16)
            cp = pltpu.make_async_copy(ostage, xe_ref.at[e, pl.ds(lo, WIN)], osem)
            cp.start()
            cp.wait()

            @pl.when(c == nwin - 1)
            def _():
                carry[e] = ostage[pl.ds(pl.multiple_of(tail, ROW_ALIGN), ROW_ALIGN), :]
            return unused

        lax.fori_loop(1, nwin, extra, 0)

    @pl.when(t == nb - 1)
    def _():
        for e in range(N_EXPERTS):
            main_copy(e, t, slot).wait()
        stage[1 - slot, 0:WIN, :] = jnp.zeros((WIN, D_MODEL), BF16)
        pads = [pltpu.make_async_copy(stage.at[1 - slot, 0:WIN], xe_ref.at[e, cap:cap + WIN], sem.at[e])
                for e in range(N_EXPERTS)]
        for cp in pads:
            cp.start()
        for cp in pads:
            cp.wait()


def _dispatch(hn, pos, offs, cap):
    b = hn.shape[0]
    nb = b * SEQ // ROUTE_TILE
    sb, tt, groups = _block_geometry(b)
    return pl.pallas_call(
        functools.partial(_dispatch_kernel, nb=nb, cap=cap),
        grid_spec=pltpu.PrefetchScalarGridSpec(
            num_scalar_prefetch=1,
            grid=(nb,),
            in_specs=[
                pl.BlockSpec((sb, tt, D_MODEL), lambda t, o: (t % groups, t // groups, 0)),
                pl.BlockSpec((N_EXPERTS, ROUTE_TILE), lambda t, o: (0, t)),
            ],
            out_specs=pl.BlockSpec(memory_space=pl.ANY),
            scratch_shapes=[
                pltpu.VMEM((2, N_EXPERTS * WIN, D_MODEL), BF16),
                pltpu.VMEM((WIN, D_MODEL), BF16),
                pltpu.VMEM((N_EXPERTS, ROW_ALIGN, D_MODEL), BF16),
                pltpu.SemaphoreType.DMA((N_EXPERTS,)),
                pltpu.SemaphoreType.DMA(()),
            ],
        ),
        out_shape=jax.ShapeDtypeStruct((N_EXPERTS, cap + WIN, D_MODEL), BF16),
        compiler_params=_cparams("arbitrary"),
        name="dispatch",
    )(offs, hn, pos)


def _ffn_kernel(xe_ref, wg_ref, wu_ref, wd_ref, ye_ref):
    x = xe_ref[0]
    g = _dot(x, wg_ref[0])
    u = _dot(x, wu_ref[0])
    hid = (g * jax.nn.sigmoid(g) * u).astype(BF16)
    ye_ref[0] = _dot(hid, wd_ref[0]).astype(BF16)


def _ffn(xe, w_gate, w_up, w_down, cap):
    til = pl.BlockSpec((1, FFN_TILE, D_MODEL), lambda e, j: (e, j, 0))
    wsp = pl.BlockSpec((1, D_MODEL, D_MODEL), lambda e, j: (e, 0, 0))
    return pl.pallas_call(
        _ffn_kernel,
        grid=(N_EXPERTS, cap // FFN_TILE),
        in_specs=[til, wsp, wsp, wsp],
        out_specs=til,
        out_shape=jax.ShapeDtypeStruct((N_EXPERTS, cap, D_MODEL), BF16),
        compiler_params=_cparams("arbitrary", "arbitrary"),
        name="expert_ffn",
    )(xe, w_gate, w_up, w_down)


def _combine_kernel(offs_ref, x_ref, pos_ref, gate_ref, mod_ref, gfin_ref, ye_ref, o_ref,
                    yin, obuf, yacc, sem, osem, *, nb, cap, final):
    t = pl.program_id(0)
    slot = t % 2
    off = lambda e, tt: offs_ref[e * (nb + 1) + tt]
    base = lambda lo: pl.multiple_of(jnp.minimum(lo, cap - WIN), ROW_ALIGN)
    slots = lax.broadcasted_iota(I32, (WIN, ROUTE_TILE), 0)

    def main_copy(e, tt, sl):
        return pltpu.make_async_copy(ye_ref.at[e, pl.ds(base(_align(off(e, tt))), WIN)],
                                     yin.at[sl, e * WIN:(e + 1) * WIN], sem.at[sl, e])

    @pl.when(t == 0)
    def _():
        for e in range(N_EXPERTS):
            main_copy(e, 0, 0).start()

    @pl.when(t + 1 < nb)
    def _():
        for e in range(N_EXPERTS):
            main_copy(e, t + 1, 1 - slot).start()

    pos = pos_ref[...]
    gate = gate_ref[...]

    def weights(e, lo):
        p = pos[e:e + 1, :]
        hit = (p - base(lo) == slots) & (p >= lo) & (p < lo + WIN)
        return jnp.where(hit, gate[e:e + 1, :], 0.0).astype(BF16)

    gmat = jnp.concatenate([weights(e, _align(off(e, t))) for e in range(N_EXPERTS)], axis=0)
    for e in range(N_EXPERTS):
        main_copy(e, t, slot).wait()
    yacc[...] = _dot_tn(gmat, yin[slot])

    for e in range(N_EXPERTS):
        start = _align(off(e, t))
        nwin = lax.shift_right_logical(jnp.maximum(off(e, t + 1) - start - 1, 0), WIN_SHIFT) + 1

        def extra(c, carry):
            lo = start + c * WIN
            cp = pltpu.make_async_copy(ye_ref.at[e, pl.ds(base(lo), WIN)], obuf, osem)
            cp.start()
            cp.wait()
            yacc[...] += _dot_tn(weights(e, lo), obuf[...])
            return carry

        lax.fori_loop(1, nwin, extra, 0)

    y = x_ref[...] + mod_ref[:, 5:6, :] * yacc[...].reshape(x_ref.shape)
    if final:
        y = _rms(y) * gfin_ref[...]
    o_ref[...] = y


def _combine(x1, pos, gate, mod, g_final, ye, offs, cap, final):
    b = x1.shape[0]
    nb = b * SEQ // ROUTE_TILE
    sb, tt, groups = _block_geometry(b)
    tok = pl.BlockSpec((sb, tt, D_MODEL), lambda t, o: (t % groups, t // groups, 0))
    rt = pl.BlockSpec((N_EXPERTS, ROUTE_TILE), lambda t, o: (0, t))
    return pl.pallas_call(
        functools.partial(_combine_kernel, nb=nb, cap=cap, final=final),
        grid_spec=pltpu.PrefetchScalarGridSpec(
            num_scalar_prefetch=1,
            grid=(nb,),
            in_specs=[
                tok, rt, rt,
                pl.BlockSpec((sb, 6, D_MODEL), lambda t, o: (t % groups, 0, 0)),
                pl.BlockSpec((1, D_MODEL), lambda t, o: (0, 0)),
                pl.BlockSpec(memory_space=pl.ANY),
            ],
            out_specs=tok,
            scratch_shapes=[
                pltpu.VMEM((2, N_EXPERTS * WIN, D_MODEL), BF16),
                pltpu.VMEM((WIN, D_MODEL), BF16),
                pltpu.VMEM((ROUTE_TILE, D_MODEL), F32),
                pltpu.SemaphoreType.DMA((2, N_EXPERTS)),
                pltpu.SemaphoreType.DMA(()),
            ],
        ),
        out_shape=jax.ShapeDtypeStruct((b, SEQ, D_MODEL), F32),
        compiler_params=_cparams("arbitrary"),
        name="combine",
    )(offs, x1, pos, gate, mod, g_final.reshape(1, D_MODEL), ye)


def _trunk(x, c, w_ada, b_ada, g_mix, w_in, sink, q_gain, k_gain, rpb, g_grp, w_out,
           g_ffn, w_router, w_gate, w_up, w_down, g_final):
    b = x.shape[0]
    n = b * SEQ
    cap = CAPACITY_FACTOR * n // N_EXPERTS
    depth = w_ada.shape[0]
    for i in range(depth):
        mod = _ada(c, w_ada[i], b_ada[i]).reshape(b, 6, D_MODEL)
        qa, ka, va, qb, kb, vb, qc, kc, vc = _inproj(x, mod, g_mix[i], w_in[i], q_gain[i], k_gain[i])
        ma = _attn_a(qa, ka, va, sink[i], g_grp[i, :A_W])
        mb = _attn_b(qb, kb, vb, g_grp[i, A_W:A_W + B_W])
        mc = _attn_c(qc, kc, vc, rpb[i], g_grp[i, A_W + B_W:])
        x1, hn, aff = _outproj(x, ma, mb, mc, mod, w_out[i], g_ffn[i], w_router[i])
        pos, gate, offs = _route(aff, cap, b)
        xe = _dispatch(hn.reshape(b, SEQ, D_MODEL), pos, offs, cap)
        ye = _ffn(xe, w_gate[i], w_up[i], w_down[i], cap)
        x = _combine(x1, pos, gate, mod, g_final, ye, offs, cap, final=(i == depth - 1))
    return x


def kernel(x_prompt, x_sample, c_prompt, c_sample, w_ada, b_ada, g_mix, w_in, sink, q_gain, k_gain, rpb, g_grp,
           w_out, g_ffn, w_router, w_gate, w_up, w_down, g_final):
    w_in, w_out, w_gate, w_up, w_down = (w.astype(BF16) for w in (w_in, w_out, w_gate, w_up, w_down))
    args = (w_ada, b_ada, g_mix, w_in, sink, q_gain, k_gain, rpb, g_grp, w_out,
            g_ffn, w_router, w_gate, w_up, w_down, g_final)
    return (_trunk(x_prompt, c_prompt, *args), _trunk(x_sample, c_sample, *args))
```

```python
import functools

import numpy as np
import jax
import jax.numpy as jnp
from jax import lax
from jax.experimental import pallas as pl
from jax.experimental.pallas import tpu as pltpu

F32 = jnp.float32
BF16 = jnp.bfloat16
I32 = jnp.int32

D_MODEL = 1024
SEQ = 2048
HEAD_DIM = 64
A_HEADS, A_KV = 8, 2
B_HEADS, B_KV = 4, 2
C_HEADS = 4
A_W, B_W, C_W = A_HEADS * HEAD_DIM, B_HEADS * HEAD_DIM, C_HEADS * HEAD_DIM
IN_WIDTH = 2048
WINDOW = 128
GRID_W = 64
GRID_ROWS = SEQ // GRID_W
NA_ROWS, NA_COLS = 8, 16
N_EXPERTS = 16
CAPACITY_FACTOR = 2
ROPE_THETA = 10000.0
EPS = 1e-6
NEG_INF = -1e30
Q_SCALE = HEAD_DIM ** -0.5

LANES = 128
ROW_ALIGN = 16
ROW_ALIGN_SHIFT = 4
TOK_TILE = 512
Q_TILE_B = 256
NBR_TILE_ROWS = 4
NBR_KEY_ROWS = NA_ROWS + NBR_TILE_ROWS - 1
ROUTE_TILE = 256
ROUTE_SEQS = 16
WIN = 64
WIN_SHIFT = 6
FFN_TILE = 512
VMEM_LIMIT = 56 * 1024 * 1024


def _cparams(*sem):
    return pltpu.CompilerParams(dimension_semantics=sem, vmem_limit_bytes=VMEM_LIMIT)


def _split_bf16(a):
    hi = a.astype(BF16)
    lo = (a - hi.astype(F32)).astype(BF16)
    return hi, lo


def _dot(a, b):
    return jnp.dot(a, b, preferred_element_type=F32)


def _dot_nt(a, b):
    return lax.dot_general(a, b, (((1,), (1,)), ((), ())), preferred_element_type=F32)


def _dot_tn(a, b):
    return lax.dot_general(a, b, (((0,), (0,)), ((), ())), preferred_element_type=F32)


def _dot3(a, w):
    ah, al = _split_bf16(a)
    wh, wl = _split_bf16(w)
    return _dot(ah, wh) + _dot(al, wh) + _dot(ah, wl)


def _rms(x):
    return x * lax.rsqrt(jnp.mean(x * x, axis=-1, keepdims=True) + EPS)


def _ada_kernel(c_ref, w_ref, b_ref, o_ref):
    c = c_ref[...]
    a = c * jax.nn.sigmoid(c)
    o_ref[...] = _dot3(a, w_ref[...]) + b_ref[...]


def _ada(c, w_ada, b_ada):
    b = c.shape[0]
    return pl.pallas_call(
        _ada_kernel,
        grid=(6,),
        in_specs=[
            pl.BlockSpec((b, D_MODEL), lambda j: (0, 0)),
            pl.BlockSpec((D_MODEL, D_MODEL), lambda j: (0, j)),
            pl.BlockSpec((1, D_MODEL), lambda j: (0, j)),
        ],
        out_specs=pl.BlockSpec((b, D_MODEL), lambda j: (0, j)),
        out_shape=jax.ShapeDtypeStruct((b, 6 * D_MODEL), F32),
        compiler_params=_cparams("arbitrary"),
        name="ada",
    )(c, w_ada, b_ada.reshape(1, 6 * D_MODEL))


def _rotate(x, cos, sin_signed, half):
    w = x.shape[1]
    lane = lax.broadcasted_iota(I32, x.shape, 1)
    first = (lane % (2 * half)) < half
    rot = jnp.where(first, pltpu.roll(x, w - half, 1), pltpu.roll(x, half, 1))
    return x * cos + rot * sin_signed


def _head_rms(x, hm, gain):
    sq = x * x
    sh, sl = _split_bf16(sq)
    ms = (_dot(sh, hm) + _dot(sl, hm)) * (1.0 / HEAD_DIM)
    return x * lax.rsqrt(ms + EPS) * gain


def _inproj_kernel(x_ref, mod_ref, g_ref, w_ref, ca_ref, sa_ref, cb_ref, sb_ref, qg_ref, kg_ref, hm_ref,
                   qa_ref, ka_ref, va_ref, qb_ref, kb_ref, vb_ref, qc_ref, kc_ref, vc_ref):
    x = x_ref[0]
    hn = _rms(x) * g_ref[...]
    hn = hn * (1.0 + mod_ref[0, 1:2, :]) + mod_ref[0, 0:1, :]
    proj = _dot(hn.astype(BF16), w_ref[...])
    ca, sa, cb, sb = ca_ref[...], sa_ref[...], cb_ref[...], sb_ref[...]
    hm = hm_ref[...]

    col = 0
    for j in range(A_W // LANES):
        qa_ref[0, :, j * LANES:(j + 1) * LANES] = (
            _rotate(proj[:, col:col + LANES], ca, sa, HEAD_DIM // 2) * Q_SCALE).astype(BF16)
        col += LANES
    ka_ref[0] = _rotate(proj[:, col:col + LANES], ca, sa, HEAD_DIM // 2).astype(BF16)
    col += LANES
    va_ref[0] = proj[:, col:col + LANES].astype(BF16)
    col += LANES
    for j in range(B_W // LANES):
        q = _head_rms(proj[:, col:col + LANES], hm, qg_ref[...])
        qb_ref[0, :, j * LANES:(j + 1) * LANES] = (_rotate(q, cb, sb, HEAD_DIM // 4) * Q_SCALE).astype(BF16)
        col += LANES
    k = _head_rms(proj[:, col:col + LANES], hm, kg_ref[...])
    kb_ref[0] = _rotate(k, cb, sb, HEAD_DIM // 4).astype(BF16)
    col += LANES
    vb_ref[0] = proj[:, col:col + LANES].astype(BF16)
    col += LANES
    qc_ref[0] = (proj[:, col:col + C_W] * Q_SCALE).astype(BF16)
    col += C_W
    kc_ref[0] = proj[:, col:col + C_W].astype(BF16)
    col += C_W
    vc_ref[0] = proj[:, col:col + C_W].astype(BF16)


def _rope_tables():
    t = np.arange(SEQ, dtype=np.float64)[:, None]
    j = np.arange(LANES)[None, :] % HEAD_DIM
    inv_a = ROPE_THETA ** (-(2.0 * (j % 32)) / HEAD_DIM)
    ang_a = t * inv_a
    sign_a = np.where(j < 32, -1.0, 1.0)
    inv_b = ROPE_THETA ** (-(2.0 * (j % 16)) / (HEAD_DIM // 2))
    pos_b = np.where(j < 32, np.floor(t / GRID_W), t % GRID_W)
    ang_b = pos_b * inv_b
    sign_b = np.where((j % 32) < 16, -1.0, 1.0)
    f = lambda a: jnp.asarray(a, dtype=F32)
    return f(np.cos(ang_a)), f(np.sin(ang_a) * sign_a), f(np.cos(ang_b)), f(np.sin(ang_b) * sign_b)


def _inproj(x, mod, g_mix, w_in, q_gain, k_gain):
    b = x.shape[0]
    ca, sa, cb, sb = _rope_tables()
    hm = jnp.asarray(np.kron(np.eye(LANES // HEAD_DIM), np.ones((HEAD_DIM, HEAD_DIM))), dtype=BF16)
    gain2 = lambda g: jnp.tile(g, LANES // HEAD_DIM).reshape(1, LANES)
    tok = lambda w: pl.BlockSpec((1, TOK_TILE, w), lambda i, j: (i, j, 0))
    tab = pl.BlockSpec((TOK_TILE, LANES), lambda i, j: (j, 0))
    full = lambda r, c: pl.BlockSpec((r, c), lambda i, j: (0, 0))
    widths = (A_W, LANES, LANES, B_W, LANES, LANES, C_W, C_W, C_W)
    return pl.pallas_call(
        _inproj_kernel,
        grid=(b, SEQ // TOK_TILE),
        in_specs=[
            tok(D_MODEL),
            pl.BlockSpec((1, 6, D_MODEL), lambda i, j: (i, 0, 0)),
            full(1, D_MODEL),
            full(D_MODEL, IN_WIDTH),
            tab, tab, tab, tab,
            full(1, LANES), full(1, LANES), full(LANES, LANES),
        ],
        out_specs=[tok(w) for w in widths],
        out_shape=[jax.ShapeDtypeStruct((b, SEQ, w), BF16) for w in widths],
        compiler_params=_cparams("arbitrary", "arbitrary"),
        name="inproj",
    )(x, mod, g_mix.reshape(1, D_MODEL), w_in, ca, sa, cb, sb, gain2(q_gain), gain2(k_gain), hm)


def _softmax_pv(s, v, extra=None):
    m = jnp.max(s, axis=-1, keepdims=True)
    if extra is not None:
        m = jnp.maximum(m, extra)
    p = jnp.exp(s - m)
    l = jnp.sum(p, axis=-1, keepdims=True)
    if extra is not None:
        l = l + jnp.exp(extra - m)
    return _dot(p.astype(BF16), v) * (1.0 / l)


def _attn_a_kernel(sink_ref, q_ref, k_ref, v_ref, g_ref, mask_ref, o_ref):
    span = WINDOW * 3
    group = A_HEADS // A_KV
    rows = group * WINDOW
    nblocks = SEQ // WINDOW
    head_of_row = lax.broadcasted_iota(I32, (rows, 1), 0) // WINDOW
    g = g_ref[...]
    sinks = []
    for kv in range(A_KV):
        sink = jnp.zeros((rows, 1), F32)
        for j in range(group):
            sink = jnp.where(head_of_row == j, sink_ref[kv * group + j], sink)
        sinks.append(sink)

    def block(i, carry):
        start = pl.multiple_of(i * WINDOW, WINDOW)
        kstart = pl.multiple_of(jnp.clip(start - WINDOW, 0, SEQ - span), WINDOW)
        q = q_ref[0, pl.ds(start, WINDOW), :]
        k = k_ref[0, pl.ds(kstart, span), :]
        v = v_ref[0, pl.ds(kstart, span), :]
        case = jnp.where(i == 0, 0, jnp.where(i == nblocks - 1, 2, 1))

        def score(kv):
            heads = range(kv * group, (kv + 1) * group)
            qs = jnp.concatenate([q[:, h * HEAD_DIM:(h + 1) * HEAD_DIM] for h in heads], axis=0)
            return _dot_nt(qs, k[:, kv * HEAD_DIM:(kv + 1) * HEAD_DIM]) + mask_ref[case]

        outs = []
        nxt = score(0)
        for kv in range(A_KV):
            cur, nxt = nxt, (score(kv + 1) if kv + 1 < A_KV else None)
            o = _softmax_pv(cur, v[:, kv * HEAD_DIM:(kv + 1) * HEAD_DIM], extra=sinks[kv])
            outs += [o[j * WINDOW:(j + 1) * WINDOW] for j in range(group)]
        o = jnp.concatenate(outs, axis=1)
        o_ref[0, pl.ds(start, WINDOW), :] = (_rms(o) * g).astype(BF16)
        return carry

    lax.fori_loop(0, SEQ // WINDOW, block, 0)


def _window_masks():
    rows = (A_HEADS // A_KV) * WINDOW
    qi = np.arange(rows)[:, None] % WINDOW
    ki = np.arange(3 * WINDOW)[None, :]
    masks = [np.where(np.abs(ki - qi - shift * WINDOW) <= WINDOW, 0.0, NEG_INF) for shift in range(3)]
    return jnp.asarray(np.stack(masks), dtype=F32)


def _attn_a(q, k, v, sink, g):
    b = q.shape[0]
    seq = lambda w: pl.BlockSpec((1, SEQ, w), lambda i: (i, 0, 0))
    masks = _window_masks()
    return pl.pallas_call(
        _attn_a_kernel,
        grid=(b,),
        in_specs=[
            pl.BlockSpec(memory_space=pltpu.SMEM),
            seq(A_W), seq(LANES), seq(LANES),
            pl.BlockSpec((1, A_W), lambda i: (0, 0)),
            pl.BlockSpec(masks.shape, lambda i: (0, 0, 0)),
        ],
        out_specs=seq(A_W),
        out_shape=jax.ShapeDtypeStruct((b, SEQ, A_W), BF16),
        compiler_params=_cparams("arbitrary"),
        name="attn_window",
    )(sink, q, k, v, g.reshape(1, A_W), masks)


def _attn_b_kernel(q_ref, k_ref, v_ref, g_ref, o_ref):
    q = q_ref[0]
    k = k_ref[0]
    v = v_ref[0]
    group = B_HEADS // B_KV
    kv_cols = lambda h: slice((h // group) * HEAD_DIM, (h // group + 1) * HEAD_DIM)
    score = lambda h: _dot_nt(q[:, h * HEAD_DIM:(h + 1) * HEAD_DIM], k[:, kv_cols(h)])
    outs = []
    nxt = score(0)
    for h in range(B_HEADS):
        cur, nxt = nxt, (score(h + 1) if h + 1 < B_HEADS else None)
        outs.append(_softmax_pv(cur, v[:, kv_cols(h)]))
    o = jnp.concatenate(outs, axis=1)
    o_ref[0] = (_rms(o) * g_ref[...]).astype(BF16)


def _attn_b(q, k, v, g):
    b = q.shape[0]
    seq = lambda w: pl.BlockSpec((1, SEQ, w), lambda i, j: (i, 0, 0))
    til = pl.BlockSpec((1, Q_TILE_B, B_W), lambda i, j: (i, j, 0))
    return pl.pallas_call(
        _attn_b_kernel,
        grid=(b, SEQ // Q_TILE_B),
        in_specs=[til, seq(LANES), seq(LANES), pl.BlockSpec((1, B_W), lambda i, j: (0, 0))],
        out_specs=til,
        out_shape=jax.ShapeDtypeStruct((b, SEQ, B_W), BF16),
        compiler_params=_cparams("arbitrary", "arbitrary"),
        name="attn_dense",
    )(q, k, v, g.reshape(1, B_W))


def _attn_c_kernel(q_ref, k_ref, v_ref, bias_ref, g_ref, o_ref):
    nq = NBR_TILE_ROWS * GRID_W
    nkeys = NBR_KEY_ROWS * GRID_W
    ntiles = GRID_ROWS // NBR_TILE_ROWS
    g = g_ref[...]

    def tile(i, carry):
        r0 = i * NBR_TILE_ROWS
        ws = jnp.clip(r0 - NA_ROWS // 2, 0, GRID_ROWS - NBR_KEY_ROWS)
        cls = jnp.where(i == 0, 0, jnp.where(i == ntiles - 1, 2, 1))
        qstart = pl.multiple_of(r0 * GRID_W, GRID_W)
        kstart = pl.multiple_of(ws * GRID_W, GRID_W)
        q = q_ref[0, pl.ds(qstart, nq), :]
        k = k_ref[0, pl.ds(kstart, nkeys), :]
        v = v_ref[0, pl.ds(kstart, nkeys), :]
        cols = lambda h: slice(h * HEAD_DIM, (h + 1) * HEAD_DIM)
        score = lambda h: _dot_nt(q[:, cols(h)], k[:, cols(h)]) + bias_ref[cls, h]
        outs = []
        nxt = score(0)
        for h in range(C_HEADS):
            cur, nxt = nxt, (score(h + 1) if h + 1 < C_HEADS else None)
            outs.append(_softmax_pv(cur, v[:, cols(h)]))
        o = jnp.concatenate(outs, axis=1)
        o_ref[0, pl.ds(qstart, nq), :] = (_rms(o) * g).astype(BF16)
        return carry

    lax.fori_loop(0, ntiles, tile, 0)


def _nbr_bias(rpb):
    edge = GRID_W - NA_COLS
    padded = jnp.pad(rpb, ((0, 0), (0, 0), (edge, edge)))
    toe = jnp.stack([padded[..., GRID_W - 1 - c:2 * GRID_W - 1 - c] for c in range(GRID_W)], axis=2)
    c = np.arange(GRID_W)[:, None]
    kc = np.arange(GRID_W)[None, :]
    cs = np.clip(c - NA_COLS // 2, 0, GRID_W - NA_COLS)
    toe = jnp.where((kc >= cs) & (kc < cs + NA_COLS), toe, NEG_INF)
    masked = jnp.full((C_HEADS, GRID_W, GRID_W), NEG_INF, F32)
    ntiles = GRID_ROWS // NBR_TILE_ROWS
    classes = []
    for r0 in (0, NBR_TILE_ROWS, (ntiles - 1) * NBR_TILE_ROWS):
        ws = min(max(r0 - NA_ROWS // 2, 0), GRID_ROWS - NBR_KEY_ROWS)
        per_query_row = []
        for rq in range(NBR_TILE_ROWS):
            r = r0 + rq
            rs = min(max(r - NA_ROWS // 2, 0), GRID_ROWS - NA_ROWS)
            blocks = [toe[:, ws + kr - r + NA_ROWS - 1] if rs <= ws + kr < rs + NA_ROWS else masked
                      for kr in range(NBR_KEY_ROWS)]
            per_query_row.append(jnp.stack(blocks, axis=2))
        classes.append(jnp.stack(per_query_row, axis=1))
    return jnp.stack(classes).reshape(3, C_HEADS, NBR_TILE_ROWS * GRID_W, NBR_KEY_ROWS * GRID_W)


def _attn_c(q, k, v, rpb, g):
    b = q.shape[0]
    seq = pl.BlockSpec((1, SEQ, C_W), lambda i: (i, 0, 0))
    return pl.pallas_call(
        _attn_c_kernel,
        grid=(b,),
        in_specs=[
            seq, seq, seq,
            pl.BlockSpec((3, C_HEADS, NBR_TILE_ROWS * GRID_W, NBR_KEY_ROWS * GRID_W), lambda i: (0, 0, 0, 0)),
            pl.BlockSpec((1, C_W), lambda i: (0, 0)),
        ],
        out_specs=seq,
        out_shape=jax.ShapeDtypeStruct((b, SEQ, C_W), BF16),
        compiler_params=_cparams("arbitrary"),
        name="attn_nbr",
    )(q, k, v, _nbr_bias(rpb), g.reshape(1, C_W))


def _outproj_kernel(x_ref, ma_ref, mb_ref, mc_ref, mod_ref, wo_ref, g_ref, wr_ref, x1_ref, hn_ref, aff_ref):
    parts = 2
    rows = x_ref.shape[1] // parts

    def mixed(p):
        r = slice(p * rows, (p + 1) * rows)
        return (_dot(ma_ref[0, r, :], wo_ref[0:A_W, :]) + _dot(mb_ref[0, r, :], wo_ref[A_W:A_W + B_W, :])
                + _dot(mc_ref[0, r, :], wo_ref[A_W + B_W:, :]))

    nxt = mixed(0)
    for p in range(parts):
        r = slice(p * rows, (p + 1) * rows)
        attn, nxt = nxt, (mixed(p + 1) if p + 1 < parts else None)
        x1 = x_ref[0, r, :] + mod_ref[0, 2:3, :] * attn
        x1_ref[0, r, :] = x1
        hn = _rms(x1) * g_ref[...]
        hn = hn * (1.0 + mod_ref[0, 4:5, :]) + mod_ref[0, 3:4, :]
        hn_ref[r, :] = hn.astype(BF16)
        logits = _dot3(hn, wr_ref[...])
        lt = logits.T[0:N_EXPERTS, :]
        e = jnp.exp(lt - jnp.max(lt, axis=0, keepdims=True))
        aff_ref[:, r] = e * (1.0 / jnp.sum(e, axis=0, keepdims=True))


def _outproj(x, ma, mb, mc, mod, w_out, g_ffn, w_router):
    b = x.shape[0]
    n = b * SEQ
    nt = SEQ // TOK_TILE
    tok = lambda w: pl.BlockSpec((1, TOK_TILE, w), lambda i, j: (i, j, 0))
    wr = jnp.pad(w_router, ((0, 0), (0, LANES - N_EXPERTS)))
    return pl.pallas_call(
        _outproj_kernel,
        grid=(b, nt),
        in_specs=[
            tok(D_MODEL), tok(A_W), tok(B_W), tok(C_W),
            pl.BlockSpec((1, 6, D_MODEL), lambda i, j: (i, 0, 0)),
            pl.BlockSpec((D_MODEL, D_MODEL), lambda i, j: (0, 0)),
            pl.BlockSpec((1, D_MODEL), lambda i, j: (0, 0)),
            pl.BlockSpec((D_MODEL, LANES), lambda i, j: (0, 0)),
        ],
        out_specs=[
            tok(D_MODEL),
            pl.BlockSpec((TOK_TILE, D_MODEL), lambda i, j: (i * nt + j, 0)),
            pl.BlockSpec((N_EXPERTS, TOK_TILE), lambda i, j: (0, i * nt + j)),
        ],
        out_shape=[
            jax.ShapeDtypeStruct((b, SEQ, D_MODEL), F32),
            jax.ShapeDtypeStruct((n, D_MODEL), BF16),
            jax.ShapeDtypeStruct((N_EXPERTS, n), F32),
        ],
        compiler_params=_cparams("arbitrary", "arbitrary"),
        name="outproj_router",
    )(x, ma, mb, mc, mod, w_out, g_ffn.reshape(1, D_MODEL), wr)


def _route_kernel(aff_ref, pos_ref, rp_ref, *, cap):
    nr = aff_ref.shape[1]
    as_bits = lambda a: lax.bitcast_convert_type(a, I32)

    def bisect(_, carry):
        lo, hi = carry
        mid = lo + ((hi - lo) >> 1)
        ge = jnp.where(as_bits(aff_ref[...]) >= mid, 1.0, 0.0)
        cnt = jnp.sum(jnp.sum(ge, axis=1, keepdims=True), axis=2, keepdims=True)
        keep = cnt >= cap
        return jnp.where(keep, mid, lo), jnp.where(keep, hi, mid)

    lo0 = jnp.zeros((N_EXPERTS, 1, 1), I32)
    hi0 = jnp.full((N_EXPERTS, 1, 1), 0x7F800000, I32)
    thr, _ = lax.fori_loop(0, 31, bisect, (lo0, hi0))

    tri = jnp.where(lax.broadcasted_iota(I32, (LANES, LANES), 0) <= lax.broadcasted_iota(I32, (LANES, LANES), 1),
                    1.0, 0.0).astype(BF16)
    below = jnp.where(lax.broadcasted_iota(I32, (nr, nr), 1) < lax.broadcasted_iota(I32, (nr, nr), 0),
                      1.0, 0.0).astype(BF16)

    def ranks(mask):
        incl = _dot(mask.astype(BF16), tri)
        before = _dot(below, incl.astype(BF16))[:, LANES - 1:]
        return incl - mask + before, before

    for e in range(N_EXPERTS):
        b = as_bits(aff_ref[e])
        t = thr[e]
        gt = b > t
        eq = b == t
        n_gt = jnp.sum(jnp.sum(jnp.where(gt, 1.0, 0.0), axis=0, keepdims=True), axis=1, keepdims=True)
        eq_rank, _ = ranks(jnp.where(eq, 1.0, 0.0))
        sel = jnp.where(gt, 1.0, jnp.where(eq, jnp.where(eq_rank < cap - n_gt, 1.0, 0.0), 0.0))
        pos, before = ranks(sel)
        pos_ref[e] = jnp.where(sel > 0.5, pos, -1.0).astype(I32)
        rp_ref[e] = before


def _block_geometry(b):
    sb = min(b, ROUTE_SEQS)
    return sb, ROUTE_TILE // sb, b // sb


def _route(aff_t, cap, b):
    n = aff_t.shape[1]
    nr = n // LANES
    sb, tt, groups = _block_geometry(b)
    aff_t = aff_t.reshape(N_EXPERTS, groups, sb, SEQ // tt, tt).transpose(0, 3, 1, 2, 4).reshape(N_EXPERTS, n)
    pos, rp = pl.pallas_call(
        functools.partial(_route_kernel, cap=cap),
        out_shape=[
            jax.ShapeDtypeStruct((N_EXPERTS, nr, LANES), I32),
            jax.ShapeDtypeStruct((N_EXPERTS, nr, 1), F32),
        ],
        compiler_params=pltpu.CompilerParams(vmem_limit_bytes=VMEM_LIMIT),
        name="route",
    )(aff_t.reshape(N_EXPERTS, nr, LANES))
    per_block = ROUTE_TILE // LANES
    offs = rp[:, ::per_block, 0].astype(I32)
    offs = jnp.concatenate([offs, jnp.full((N_EXPERTS, 1), cap, I32)], axis=1)
    return pos.reshape(N_EXPERTS, n), aff_t, offs.reshape(-1)


def _align(v):
    return pl.multiple_of(lax.shift_left(lax.shift_right_logical(v, ROW_ALIGN_SHIFT), ROW_ALIGN_SHIFT), ROW_ALIGN)


def _dispatch_kernel(offs_ref, hn_ref, pos_ref, xe_ref, stage, ostage, carry, sem, osem, *, nb, cap):
    t = pl.program_id(0)
    slot = t % 2
    off = lambda e, tt: offs_ref[e * (nb + 1) + tt]
    rows = lax.broadcasted_iota(I32, (WIN, ROUTE_TILE), 0)
    pos = pos_ref[...]
    hn = hn_ref[...].reshape(ROUTE_TILE, D_MODEL)

    def main_copy(e, tt, sl):
        return pltpu.make_async_copy(stage.at[sl, e * WIN:(e + 1) * WIN],
                                     xe_ref.at[e, pl.ds(_align(off(e, tt)), WIN)], sem.at[e])

    @pl.when(t == 0)
    def _():
        carry[...] = jnp.zeros_like(carry)

    onehot = jnp.concatenate(
        [jnp.where(pos[e:e + 1, :] - _align(off(e, t)) == rows, 1.0, 0.0) for e in range(N_EXPERTS)],
        axis=0).astype(BF16)
    stage[slot] = _dot(onehot, hn).astype(BF16)

    for e in range(N_EXPERTS):
        start = _align(off(e, t))
        filled = off(e, t + 1) - start
        nwin = lax.shift_right_logical(filled, WIN_SHIFT) + 1
        tail = _align(filled) - (nwin - 1) * WIN
        first = stage[slot, e * WIN:e * WIN + ROW_ALIGN, :]
        stage[slot, e * WIN:e * WIN + ROW_ALIGN, :] = first + carry[e]

        @pl.when(t > 0)
        def _():
            main_copy(e, t - 1, 1 - slot).wait()
        main_copy(e, t, slot).start()

        @pl.when(nwin == 1)
        def _():
            carry[e] = stage[slot, pl.ds(pl.multiple_of(e * WIN + tail, ROW_ALIGN), ROW_ALIGN), :]

        def extra(c, unused):
            lo = pl.multiple_of(start + c * WIN, ROW_ALIGN)
            win_rows = lax.broadcasted_iota(I32, (WIN, ROUTE_TILE), 0)
            oh = jnp.where(pos_ref[e:e + 1, :] - lo == win_rows, 1.0, 0.0).astype(BF16)
            ostage[...] = _dot(oh, hn_ref[...].reshape(ROUTE_TILE, D_MODEL)).astype(BF16)
            cp = pltpu.make_async_copy(ostage, xe_ref.at[e, pl.ds(lo, WIN)], osem)
            cp.start()
            cp.wait()

            @pl.when(c == nwin - 1)
            def _():
                carry[e] = ostage[pl.ds(pl.multiple_of(tail, ROW_ALIGN), ROW_ALIGN), :]
            return unused

        lax.fori_loop(1, nwin, extra, 0)

    @pl.when(t == nb - 1)
    def _():
        for e in range(N_EXPERTS):
            main_copy(e, t, slot).wait()
        stage[1 - slot, 0:WIN, :] = jnp.zeros((WIN, D_MODEL), BF16)
        pads = [pltpu.make_async_copy(stage.at[1 - slot, 0:WIN], xe_ref.at[e, cap:cap + WIN], sem.at[e])
                for e in range(N_EXPERTS)]
        for cp in pads:
            cp.start()
        for cp in pads:
            cp.wait()


def _dispatch(hn, pos, offs, cap):
    b = hn.shape[0]
    nb = b * SEQ // ROUTE_TILE
    sb, tt, groups = _block_geometry(b)
    return pl.pallas_call(
        functools.partial(_dispatch_kernel, nb=nb, cap=cap),
        grid_spec=pltpu.PrefetchScalarGridSpec(
            num_scalar_prefetch=1,
            grid=(nb,),
            in_specs=[
                pl.BlockSpec((sb, tt, D_MODEL), lambda t, o: (t % groups, t // groups, 0)),
                pl.BlockSpec((N_EXPERTS, ROUTE_TILE), lambda t, o: (0, t)),
            ],
            out_specs=pl.BlockSpec(memory_space=pl.ANY),
            scratch_shapes=[
                pltpu.VMEM((2, N_EXPERTS * WIN, D_MODEL), BF16),
                pltpu.VMEM((WIN, D_MODEL), BF16),
                pltpu.VMEM((N_EXPERTS, ROW_ALIGN, D_MODEL), BF16),
                pltpu.SemaphoreType.DMA((N_EXPERTS,)),
                pltpu.SemaphoreType.DMA(()),
            ],
        ),
        out_shape=jax.ShapeDtypeStruct((N_EXPERTS, cap + WIN, D_MODEL), BF16),
        compiler_params=_cparams("arbitrary"),
        name="dispatch",
    )(offs, hn, pos)


def _ffn_kernel(xe_ref, wg_ref, wu_ref, wd_ref, ye_ref):
    x = xe_ref[0]
    g = _dot(x, wg_ref[0])
    u = _dot(x, wu_ref[0])
    hid = (g * jax.nn.sigmoid(g) * u).astype(BF16)
    ye_ref[0] = _dot(hid, wd_ref[0]).astype(BF16)


def _ffn(xe, w_gate, w_up, w_down, cap):
    til = pl.BlockSpec((1, FFN_TILE, D_MODEL), lambda e, j: (e, j, 0))
    wsp = pl.BlockSpec((1, D_MODEL, D_MODEL), lambda e, j: (e, 0, 0))
    return pl.pallas_call(
        _ffn_kernel,
        grid=(N_EXPERTS, cap // FFN_TILE),
        in_specs=[til, wsp, wsp, wsp],
        out_specs=til,
        out_shape=jax.ShapeDtypeStruct((N_EXPERTS, cap, D_MODEL), BF16),
        compiler_params=_cparams("arbitrary", "arbitrary"),
        name="expert_ffn",
    )(xe, w_gate, w_up, w_down)


def _combine_kernel(offs_ref, x_ref, pos_ref, gate_ref, mod_ref, gfin_ref, ye_ref, o_ref,
                    yin, obuf, yacc, sem, osem, *, nb, cap, final):
    t = pl.program_id(0)
    slot = t % 2
    off = lambda e, tt: offs_ref[e * (nb + 1) + tt]
    base = lambda lo: pl.multiple_of(jnp.minimum(lo, cap - WIN), ROW_ALIGN)
    slots = lax.broadcasted_iota(I32, (WIN, ROUTE_TILE), 0)

    def main_copy(e, tt, sl):
        return pltpu.make_async_copy(ye_ref.at[e, pl.ds(base(_align(off(e, tt))), WIN)],
                                     yin.at[sl, e * WIN:(e + 1) * WIN], sem.at[sl, e])

    @pl.when(t == 0)
    def _():
        for e in range(N_EXPERTS):
            main_copy(e, 0, 0).start()

    @pl.when(t + 1 < nb)
    def _():
        for e in range(N_EXPERTS):
            main_copy(e, t + 1, 1 - slot).start()

    pos = pos_ref[...]
    gate = gate_ref[...]

    def weights(e, lo):
        p = pos[e:e + 1, :]
        hit = (p - base(lo) == slots) & (p >= lo) & (p < lo + WIN)
        return jnp.where(hit, gate[e:e + 1, :], 0.0).astype(BF16)

    gmat = jnp.concatenate([weights(e, _align(off(e, t))) for e in range(N_EXPERTS)], axis=0)
    for e in range(N_EXPERTS):
        main_copy(e, t, slot).wait()
    yacc[...] = _dot_tn(gmat, yin[slot])

    for e in range(N_EXPERTS):
        start = _align(off(e, t))
        nwin = lax.shift_right_logical(jnp.maximum(off(e, t + 1) - start - 1, 0), WIN_SHIFT) + 1

        def extra(c, carry):
            lo = start + c * WIN
            cp = pltpu.make_async_copy(ye_ref.at[e, pl.ds(base(lo), WIN)], obuf, osem)
            cp.start()
            cp.wait()
            yacc[...] += _dot_tn(weights(e, lo), obuf[...])
            return carry

        lax.fori_loop(1, nwin, extra, 0)

    y = x_ref[...] + mod_ref[:, 5:6, :] * yacc[...].reshape(x_ref.shape)
    if final:
        y = _rms(y) * gfin_ref[...]
    o_ref[...] = y


def _combine(x1, pos, gate, mod, g_final, ye, offs, cap, final):
    b = x1.shape[0]
    nb = b * SEQ // ROUTE_TILE
    sb, tt, groups = _block_geometry(b)
    tok = pl.BlockSpec((sb, tt, D_MODEL), lambda t, o: (t % groups, t // groups, 0))
    rt = pl.BlockSpec((N_EXPERTS, ROUTE_TILE), lambda t, o: (0, t))
    return pl.pallas_call(
        functools.partial(_combine_kernel, nb=nb, cap=cap, final=final),
        grid_spec=pltpu.PrefetchScalarGridSpec(
            num_scalar_prefetch=1,
            grid=(nb,),
            in_specs=[
                tok, rt, rt,
                pl.BlockSpec((sb, 6, D_MODEL), lambda t, o: (t % groups, 0, 0)),
                pl.BlockSpec((1, D_MODEL), lambda t, o: (0, 0)),
                pl.BlockSpec(memory_space=pl.ANY),
            ],
            out_specs=tok,
            scratch_shapes=[
                pltpu.VMEM((2, N_EXPERTS * WIN, D_MODEL), BF16),
                pltpu.VMEM((WIN, D_MODEL), BF16),
                pltpu.VMEM((ROUTE_TILE, D_MODEL), F32),
                pltpu.SemaphoreType.DMA((2, N_EXPERTS)),
                pltpu.SemaphoreType.DMA(()),
            ],
        ),
        out_shape=jax.ShapeDtypeStruct((b, SEQ, D_MODEL), F32),
        compiler_params=_cparams("arbitrary"),
        name="combine",
    )(offs, x1, pos, gate, mod, g_final.reshape(1, D_MODEL), ye)


def _trunk(x, c, w_ada, b_ada, g_mix, w_in, sink, q_gain, k_gain, rpb, g_grp, w_out,
           g_ffn, w_router, w_gate, w_up, w_down, g_final):
    b = x.shape[0]
    n = b * SEQ
    cap = CAPACITY_FACTOR * n // N_EXPERTS
    depth = w_ada.shape[0]
    for i in range(depth):
        mod = _ada(c, w_ada[i], b_ada[i]).reshape(b, 6, D_MODEL)
        qa, ka, va, qb, kb, vb, qc, kc, vc = _inproj(x, mod, g_mix[i], w_in[i], q_gain[i], k_gain[i])
        ma = _attn_a(qa, ka, va, sink[i], g_grp[i, :A_W])
        mb = _attn_b(qb, kb, vb, g_grp[i, A_W:A_W + B_W])
        mc = _attn_c(qc, kc, vc, rpb[i], g_grp[i, A_W + B_W:])
        x1, hn, aff = _outproj(x, ma, mb, mc, mod, w_out[i], g_ffn[i], w_router[i])
        pos, gate, offs = _route(aff, cap, b)
        xe = _dispatch(hn.reshape(b, SEQ, D_MODEL), pos, offs, cap)
        ye = _ffn(xe, w_gate[i], w_up[i], w_down[i], cap)
        x = _combine(x1, pos, gate, mod, g_final, ye, offs, cap, final=(i == depth - 1))
    return x


def kernel(x_prompt, x_sample, c_prompt, c_sample, w_ada, b_ada, g_mix, w_in, sink, q_gain, k_gain, rpb, g_grp,
           w_out, g_ffn, w_router, w_gate, w_up, w_down, g_final):
    w_in, w_out, w_gate, w_up, w_down = (w.astype(BF16) for w in (w_in, w_out, w_gate, w_up, w_down))
    args = (w_ada, b_ada, g_mix, w_in, sink, q_gain, k_gain, rpb, g_grp, w_out,
            g_ffn, w_router, w_gate, w_up, w_down, g_final)
    return (_trunk(x_prompt, c_prompt, *args), _trunk(x_sample, c_sample, *args))
```

```python
import functools

import numpy as np
import jax
import jax.numpy as jnp
from jax import lax
from jax.experimental import pallas as pl
from jax.experimental.pallas import tpu as pltpu

F32 = jnp.float32
BF16 = jnp.bfloat16
I32 = jnp.int32

D_MODEL = 1024
SEQ = 2048
HEAD_DIM = 64
A_HEADS, A_KV = 8, 2
B_HEADS, B_KV = 4, 2
C_HEADS = 4
A_W, B_W, C_W = A_HEADS * HEAD_DIM, B_HEADS * HEAD_DIM, C_HEADS * HEAD_DIM
IN_WIDTH = 2048
WINDOW = 128
GRID_W = 64
GRID_ROWS = SEQ // GRID_W
NA_ROWS, NA_COLS = 8, 16
N_EXPERTS = 16
CAPACITY_FACTOR = 2
ROPE_THETA = 10000.0
EPS = 1e-6
NEG_INF = -1e30
Q_SCALE = HEAD_DIM ** -0.5

LANES = 128
ROW_ALIGN = 16
ROW_ALIGN_SHIFT = 4
TOK_TILE = 512
Q_TILE_B = 256
NBR_TILE_ROWS = 4
NBR_KEY_ROWS = NA_ROWS + NBR_TILE_ROWS - 1
ROUTE_TILE = 256
ROUTE_SEQS = 16
WIN = 64
WIN_SHIFT = 6
FFN_TILE = 1024
VMEM_LIMIT = 56 * 1024 * 1024


def _cparams(*sem):
    return pltpu.CompilerParams(dimension_semantics=sem, vmem_limit_bytes=VMEM_LIMIT)


def _split_bf16(a):
    hi = a.astype(BF16)
    lo = (a - hi.astype(F32)).astype(BF16)
    return hi, lo


def _dot(a, b):
    return jnp.dot(a, b, preferred_element_type=F32)


def _dot_nt(a, b):
    return lax.dot_general(a, b, (((1,), (1,)), ((), ())), preferred_element_type=F32)


def _dot_tn(a, b):
    return lax.dot_general(a, b, (((0,), (0,)), ((), ())), preferred_element_type=F32)


def _dot3(a, w):
    ah, al = _split_bf16(a)
    wh, wl = _split_bf16(w)
    return _dot(ah, wh) + _dot(al, wh) + _dot(ah, wl)


def _rms(x):
    return x * lax.rsqrt(jnp.mean(x * x, axis=-1, keepdims=True) + EPS)


def _ada_kernel(c_ref, w_ref, b_ref, o_ref):
    c = c_ref[...]
    a = c * jax.nn.sigmoid(c)
    o_ref[...] = _dot3(a, w_ref[0]) + b_ref[...]


def _ada(c, w_ada, b_ada, layer):
    b = c.shape[0]
    return pl.pallas_call(
        _ada_kernel,
        grid=(6,),
        in_specs=[
            pl.BlockSpec((b, D_MODEL), lambda j: (0, 0)),
            pl.BlockSpec((1, D_MODEL, D_MODEL), lambda j: (layer, 0, j)),
            pl.BlockSpec((1, D_MODEL), lambda j: (0, j)),
        ],
        out_specs=pl.BlockSpec((b, D_MODEL), lambda j: (0, j)),
        out_shape=jax.ShapeDtypeStruct((b, 6 * D_MODEL), F32),
        compiler_params=_cparams("arbitrary"),
        name="ada",
    )(c, w_ada, b_ada.reshape(1, 6 * D_MODEL))


def _rotate(x, cos, sin_signed, half):
    w = x.shape[1]
    lane = lax.broadcasted_iota(I32, x.shape, 1)
    first = (lane % (2 * half)) < half
    rot = jnp.where(first, pltpu.roll(x, w - half, 1), pltpu.roll(x, half, 1))
    return x * cos + rot * sin_signed


def _head_rms(x, hm, gain):
    sq = x * x
    sh, sl = _split_bf16(sq)
    ms = (_dot(sh, hm) + _dot(sl, hm)) * (1.0 / HEAD_DIM)
    return x * lax.rsqrt(ms + EPS) * gain


def _store_values(v_ref, v):
    low = lax.broadcasted_iota(I32, (v.shape[0], LANES), 1) < HEAD_DIM
    for h in range(v.shape[1] // HEAD_DIM):
        pair = v[:, (h // 2) * LANES:(h // 2 + 1) * LANES]
        vals = pair if h % 2 == 0 else pltpu.roll(pair, HEAD_DIM, 1)
        v_ref[0, :, h * LANES:(h + 1) * LANES] = jnp.where(low, vals, 1.0).astype(BF16)


def _inproj_kernel(x_ref, mod_ref, g_ref, w_ref, ca_ref, sa_ref, cb_ref, sb_ref, qg_ref, kg_ref, hm_ref,
                   qa_ref, ka_ref, va_ref, qb_ref, kb_ref, vb_ref, qc_ref, kc_ref, vc_ref):
    x = x_ref[0]
    hn = _rms(x) * g_ref[...]
    hn = hn * (1.0 + mod_ref[0, 1:2, :]) + mod_ref[0, 0:1, :]
    proj = _dot(hn.astype(BF16), w_ref[0])
    ca, sa, cb, sb = ca_ref[...], sa_ref[...], cb_ref[...], sb_ref[...]
    hm = hm_ref[...]

    col = 0
    for j in range(A_W // LANES):
        qa_ref[0, :, j * LANES:(j + 1) * LANES] = (
            _rotate(proj[:, col:col + LANES], ca, sa, HEAD_DIM // 2) * Q_SCALE).astype(BF16)
        col += LANES
    ka_ref[0] = _rotate(proj[:, col:col + LANES], ca, sa, HEAD_DIM // 2).astype(BF16)
    col += LANES
    _store_values(va_ref, proj[:, col:col + LANES])
    col += LANES
    for j in range(B_W // LANES):
        q = _head_rms(proj[:, col:col + LANES], hm, qg_ref[...])
        qb_ref[0, :, j * LANES:(j + 1) * LANES] = (_rotate(q, cb, sb, HEAD_DIM // 4) * Q_SCALE).astype(BF16)
        col += LANES
    k = _head_rms(proj[:, col:col + LANES], hm, kg_ref[...])
    kb_ref[0] = _rotate(k, cb, sb, HEAD_DIM // 4).astype(BF16)
    col += LANES
    _store_values(vb_ref, proj[:, col:col + LANES])
    col += LANES
    qc_ref[0] = (proj[:, col:col + C_W] * Q_SCALE).astype(BF16)
    col += C_W
    kc_ref[0] = proj[:, col:col + C_W].astype(BF16)
    col += C_W
    _store_values(vc_ref, proj[:, col:col + C_W])


def _rope_tables():
    t = np.arange(SEQ, dtype=np.float64)[:, None]
    j = np.arange(LANES)[None, :] % HEAD_DIM
    inv_a = ROPE_THETA ** (-(2.0 * (j % 32)) / HEAD_DIM)
    ang_a = t * inv_a
    sign_a = np.where(j < 32, -1.0, 1.0)
    inv_b = ROPE_THETA ** (-(2.0 * (j % 16)) / (HEAD_DIM // 2))
    pos_b = np.where(j < 32, np.floor(t / GRID_W), t % GRID_W)
    ang_b = pos_b * inv_b
    sign_b = np.where((j % 32) < 16, -1.0, 1.0)
    f = lambda a: jnp.asarray(a, dtype=F32)
    return f(np.cos(ang_a)), f(np.sin(ang_a) * sign_a), f(np.cos(ang_b)), f(np.sin(ang_b) * sign_b)


def _inproj(x, mod, g_mix, w_in, layer, q_gain, k_gain):
    b = x.shape[0]
    ca, sa, cb, sb = _rope_tables()
    hm = jnp.asarray(np.kron(np.eye(LANES // HEAD_DIM), np.ones((HEAD_DIM, HEAD_DIM))), dtype=BF16)
    gain2 = lambda g: jnp.tile(g, LANES // HEAD_DIM).reshape(1, LANES)
    tok = lambda w: pl.BlockSpec((1, TOK_TILE, w), lambda i, j: (i, j, 0))
    tab = pl.BlockSpec((TOK_TILE, LANES), lambda i, j: (j, 0))
    full = lambda r, c: pl.BlockSpec((r, c), lambda i, j: (0, 0))
    widths = (A_W, LANES, A_KV * LANES, B_W, LANES, B_KV * LANES, C_W, C_W, C_HEADS * LANES)
    return pl.pallas_call(
        _inproj_kernel,
        grid=(b, SEQ // TOK_TILE),
        in_specs=[
            tok(D_MODEL),
            pl.BlockSpec((1, 6, D_MODEL), lambda i, j: (i, 0, 0)),
            full(1, D_MODEL),
            pl.BlockSpec((1, D_MODEL, IN_WIDTH), lambda i, j: (layer, 0, 0)),
            tab, tab, tab, tab,
            full(1, LANES), full(1, LANES), full(LANES, LANES),
        ],
        out_specs=[tok(w) for w in widths],
        out_shape=[jax.ShapeDtypeStruct((b, SEQ, w), BF16) for w in widths],
        compiler_params=_cparams("arbitrary", "arbitrary"),
        name="inproj",
    )(x, mod, g_mix.reshape(1, D_MODEL), w_in, ca, sa, cb, sb, gain2(q_gain), gain2(k_gain), hm)


def _softmax_pv_aug(s, v_aug, extra=None):
    m = jnp.max(s, axis=-1, keepdims=True)
    if extra is not None:
        m = jnp.maximum(m, extra)
    p = jnp.exp((s - m).astype(BF16))
    o = _dot(p, v_aug)
    l = o[:, HEAD_DIM:HEAD_DIM + 1]
    if extra is not None:
        l = l + jnp.exp(extra - m)
    return o[:, 0:HEAD_DIM] * (1.0 / l)


def _attn_a_kernel(sink_ref, q_ref, k_ref, v_ref, g_ref, mask_ref, o_ref):
    span = WINDOW * 3
    group = A_HEADS // A_KV
    rows = group * WINDOW
    nblocks = SEQ // WINDOW
    head_of_row = lax.broadcasted_iota(I32, (rows, 1), 0) // WINDOW
    g = g_ref[...]
    sinks = []
    for kv in range(A_KV):
        sink = jnp.zeros((rows, 1), F32)
        for j in range(group):
            sink = jnp.where(head_of_row == j, sink_ref[kv * group + j], sink)
        sinks.append(sink)

    def block(i, carry):
        start = pl.multiple_of(i * WINDOW, WINDOW)
        kstart = pl.multiple_of(jnp.clip(start - WINDOW, 0, SEQ - span), WINDOW)
        q = q_ref[0, pl.ds(start, WINDOW), :]
        k = k_ref[0, pl.ds(kstart, span), :]
        v = v_ref[0, pl.ds(kstart, span), :]
        case = jnp.where(i == 0, 0, jnp.where(i == nblocks - 1, 2, 1))

        def score(kv):
            heads = range(kv * group, (kv + 1) * group)
            qs = jnp.concatenate([q[:, h * HEAD_DIM:(h + 1) * HEAD_DIM] for h in heads], axis=0)
            return _dot_nt(qs, k[:, kv * HEAD_DIM:(kv + 1) * HEAD_DIM]) + mask_ref[case]

        outs = []
        nxt = score(0)
        for kv in range(A_KV):
            cur, nxt = nxt, (score(kv + 1) if kv + 1 < A_KV else None)
            o = _softmax_pv_aug(cur, v[:, kv * LANES:(kv + 1) * LANES], extra=sinks[kv])
            outs += [o[j * WINDOW:(j + 1) * WINDOW] for j in range(group)]
        o = jnp.concatenate(outs, axis=1)
        o_ref[0, pl.ds(start, WINDOW), :] = (_rms(o) * g).astype(BF16)
        return carry

    lax.fori_loop(0, SEQ // WINDOW, block, 0)


def _window_masks():
    rows = (A_HEADS // A_KV) * WINDOW
    qi = np.arange(rows)[:, None] % WINDOW
    ki = np.arange(3 * WINDOW)[None, :]
    masks = [np.where(np.abs(ki - qi - shift * WINDOW) <= WINDOW, 0.0, NEG_INF) for shift in range(3)]
    return jnp.asarray(np.stack(masks), dtype=F32)


def _attn_a(q, k, v, sink, g):
    b = q.shape[0]
    seq = lambda w: pl.BlockSpec((1, SEQ, w), lambda i: (i, 0, 0))
    masks = _window_masks()
    return pl.pallas_call(
        _attn_a_kernel,
        grid=(b,),
        in_specs=[
            pl.BlockSpec(memory_space=pltpu.SMEM),
            seq(A_W), seq(LANES), seq(A_KV * LANES),
            pl.BlockSpec((1, A_W), lambda i: (0, 0)),
            pl.BlockSpec(masks.shape, lambda i: (0, 0, 0)),
        ],
        out_specs=seq(A_W),
        out_shape=jax.ShapeDtypeStruct((b, SEQ, A_W), BF16),
        compiler_params=_cparams("arbitrary"),
        name="attn_window",
    )(sink, q, k, v, g.reshape(1, A_W), masks)


def _attn_b_kernel(q_ref, k_ref, v_ref, g_ref, o_ref):
    q = q_ref[0]
    k = k_ref[0]
    v = v_ref[0]
    group = B_HEADS // B_KV
    kv_cols = lambda h: slice((h // group) * HEAD_DIM, (h // group + 1) * HEAD_DIM)
    kv_block = lambda h: slice((h // group) * LANES, (h // group + 1) * LANES)
    score = lambda h: _dot_nt(q[:, h * HEAD_DIM:(h + 1) * HEAD_DIM], k[:, kv_cols(h)])
    outs = []
    nxt = score(0)
    for h in range(B_HEADS):
        cur, nxt = nxt, (score(h + 1) if h + 1 < B_HEADS else None)
        outs.append(_softmax_pv_aug(cur, v[:, kv_block(h)]))
    o = jnp.concatenate(outs, axis=1)
    o_ref[0] = (_rms(o) * g_ref[...]).astype(BF16)


def _attn_b(q, k, v, g):
    b = q.shape[0]
    seq = lambda w: pl.BlockSpec((1, SEQ, w), lambda i, j: (i, 0, 0))
    til = pl.BlockSpec((1, Q_TILE_B, B_W), lambda i, j: (i, j, 0))
    return pl.pallas_call(
        _attn_b_kernel,
        grid=(b, SEQ // Q_TILE_B),
        in_specs=[til, seq(LANES), seq(B_KV * LANES), pl.BlockSpec((1, B_W), lambda i, j: (0, 0))],
        out_specs=til,
        out_shape=jax.ShapeDtypeStruct((b, SEQ, B_W), BF16),
        compiler_params=_cparams("arbitrary", "arbitrary"),
        name="attn_dense",
    )(q, k, v, g.reshape(1, B_W))


def _attn_c_kernel(q_ref, k_ref, v_ref, bias_ref, g_ref, o_ref):
    nq = NBR_TILE_ROWS * GRID_W
    nkeys = NBR_KEY_ROWS * GRID_W
    ntiles = GRID_ROWS // NBR_TILE_ROWS
    g = g_ref[...]

    def tile(i, carry):
        r0 = i * NBR_TILE_ROWS
        ws = jnp.clip(r0 - NA_ROWS // 2, 0, GRID_ROWS - NBR_KEY_ROWS)
        cls = jnp.where(i == 0, 0, jnp.where(i == ntiles - 1, 2, 1))
        qstart = pl.multiple_of(r0 * GRID_W, GRID_W)
        kstart = pl.multiple_of(ws * GRID_W, GRID_W)
        q = q_ref[0, pl.ds(qstart, nq), :]
        k = k_ref[0, pl.ds(kstart, nkeys), :]
        v = v_ref[0, pl.ds(kstart, nkeys), :]
        cols = lambda h: slice(h * HEAD_DIM, (h + 1) * HEAD_DIM)
        score = lambda h: _dot_nt(q[:, cols(h)], k[:, cols(h)]) + bias_ref[cls, h]
        outs = []
        nxt = score(0)
        for h in range(C_HEADS):
            cur, nxt = nxt, (score(h + 1) if h + 1 < C_HEADS else None)
            outs.append(_softmax_pv_aug(cur, v[:, h * LANES:(h + 1) * LANES]))
        o = jnp.concatenate(outs, axis=1)
        o_ref[0, pl.ds(qstart, nq), :] = (_rms(o) * g).astype(BF16)
        return carry

    lax.fori_loop(0, ntiles, tile, 0)


def _nbr_bias(rpb):
    edge = GRID_W - NA_COLS
    padded = jnp.pad(rpb, ((0, 0), (0, 0), (edge, edge)))
    toe = jnp.stack([padded[..., GRID_W - 1 - c:2 * GRID_W - 1 - c] for c in range(GRID_W)], axis=2)
    c = np.arange(GRID_W)[:, None]
    kc = np.arange(GRID_W)[None, :]
    cs = np.clip(c - NA_COLS // 2, 0, GRID_W - NA_COLS)
    toe = jnp.where((kc >= cs) & (kc < cs + NA_COLS), toe, NEG_INF)
    masked = jnp.full((C_HEADS, GRID_W, GRID_W), NEG_INF, F32)
    ntiles = GRID_ROWS // NBR_TILE_ROWS
    classes = []
    for r0 in (0, NBR_TILE_ROWS, (ntiles - 1) * NBR_TILE_ROWS):
        ws = min(max(r0 - NA_ROWS // 2, 0), GRID_ROWS - NBR_KEY_ROWS)
        per_query_row = []
        for rq in range(NBR_TILE_ROWS):
            r = r0 + rq
            rs = min(max(r - NA_ROWS // 2, 0), GRID_ROWS - NA_ROWS)
            blocks = [toe[:, ws + kr - r + NA_ROWS - 1] if rs <= ws + kr < rs + NA_ROWS else masked
                      for kr in range(NBR_KEY_ROWS)]
            per_query_row.append(jnp.stack(blocks, axis=2))
        classes.append(jnp.stack(per_query_row, axis=1))
    return jnp.stack(classes).reshape(3, C_HEADS, NBR_TILE_ROWS * GRID_W, NBR_KEY_ROWS * GRID_W)


def _attn_c(q, k, v, bias, g):
    b = q.shape[0]
    seq = pl.BlockSpec((1, SEQ, C_W), lambda i: (i, 0, 0))
    return pl.pallas_call(
        _attn_c_kernel,
        grid=(b,),
        in_specs=[
            seq, seq, pl.BlockSpec((1, SEQ, C_HEADS * LANES), lambda i: (i, 0, 0)),
            pl.BlockSpec((3, C_HEADS, NBR_TILE_ROWS * GRID_W, NBR_KEY_ROWS * GRID_W), lambda i: (0, 0, 0, 0)),
            pl.BlockSpec((1, C_W), lambda i: (0, 0)),
        ],
        out_specs=seq,
        out_shape=jax.ShapeDtypeStruct((b, SEQ, C_W), BF16),
        compiler_params=_cparams("arbitrary"),
        name="attn_nbr",
    )(q, k, v, bias, g.reshape(1, C_W))


def _outproj_kernel(x_ref, ma_ref, mb_ref, mc_ref, mod_ref, wo_ref, g_ref, wr_ref, x1_ref, hn_ref, aff_ref):
    parts = 2
    rows = x_ref.shape[1] // parts

    def mixed(p):
        r = slice(p * rows, (p + 1) * rows)
        return (_dot(ma_ref[0, r, :], wo_ref[0, 0:A_W, :]) + _dot(mb_ref[0, r, :], wo_ref[0, A_W:A_W + B_W, :])
                + _dot(mc_ref[0, r, :], wo_ref[0, A_W + B_W:, :]))

    nxt = mixed(0)
    for p in range(parts):
        r = slice(p * rows, (p + 1) * rows)
        attn, nxt = nxt, (mixed(p + 1) if p + 1 < parts else None)
        x1 = x_ref[0, r, :] + mod_ref[0, 2:3, :] * attn
        x1_ref[0, r, :] = x1
        hn = _rms(x1) * g_ref[...]
        hn = hn * (1.0 + mod_ref[0, 4:5, :]) + mod_ref[0, 3:4, :]
        hn_ref[r, :] = hn.astype(BF16)
        logits = _dot3(hn, wr_ref[...])
        lt = logits.T[0:N_EXPERTS, :]
        e = jnp.exp(lt - jnp.max(lt, axis=0, keepdims=True))
        aff_ref[:, r] = e * (1.0 / jnp.sum(e, axis=0, keepdims=True))


def _outproj(x, ma, mb, mc, mod, w_out, layer, g_ffn, w_router):
    b = x.shape[0]
    n = b * SEQ
    nt = SEQ // TOK_TILE
    tok = lambda w: pl.BlockSpec((1, TOK_TILE, w), lambda i, j: (i, j, 0))
    wr = jnp.pad(w_router, ((0, 0), (0, LANES - N_EXPERTS)))
    return pl.pallas_call(
        _outproj_kernel,
        grid=(b, nt),
        in_specs=[
            tok(D_MODEL), tok(A_W), tok(B_W), tok(C_W),
            pl.BlockSpec((1, 6, D_MODEL), lambda i, j: (i, 0, 0)),
            pl.BlockSpec((1, D_MODEL, D_MODEL), lambda i, j: (layer, 0, 0)),
            pl.BlockSpec((1, D_MODEL), lambda i, j: (0, 0)),
            pl.BlockSpec((D_MODEL, LANES), lambda i, j: (0, 0)),
        ],
        out_specs=[
            tok(D_MODEL),
            pl.BlockSpec((TOK_TILE, D_MODEL), lambda i, j: (i * nt + j, 0)),
            pl.BlockSpec((N_EXPERTS, TOK_TILE), lambda i, j: (0, i * nt + j)),
        ],
        out_shape=[
            jax.ShapeDtypeStruct((b, SEQ, D_MODEL), F32),
            jax.ShapeDtypeStruct((n, D_MODEL), BF16),
            jax.ShapeDtypeStruct((N_EXPERTS, n), F32),
        ],
        compiler_params=_cparams("arbitrary", "arbitrary"),
        name="outproj_router",
    )(x, ma, mb, mc, mod, w_out, g_ffn.reshape(1, D_MODEL), wr)


def _route_kernel(aff_ref, pos_ref, rp_ref, *, cap):
    nr = aff_ref.shape[1]
    as_bits = lambda a: lax.bitcast_convert_type(a, I32)

    def bisect(_, carry):
        lo, hi = carry
        mid = lo + ((hi - lo) >> 1)
        ge = jnp.where(as_bits(aff_ref[...]) >= mid, 1.0, 0.0)
        cnt = jnp.sum(jnp.sum(ge, axis=1, keepdims=True), axis=2, keepdims=True)
        keep = cnt >= cap
        return jnp.where(keep, mid, lo), jnp.where(keep, hi, mid)

    lo0 = jnp.zeros((N_EXPERTS, 1, 1), I32)
    hi0 = jnp.full((N_EXPERTS, 1, 1), 0x7F800000, I32)
    thr, _ = lax.fori_loop(0, 31, bisect, (lo0, hi0))

    tri = jnp.where(lax.broadcasted_iota(I32, (LANES, LANES), 0) <= lax.broadcasted_iota(I32, (LANES, LANES), 1),
                    1.0, 0.0).astype(BF16)
    below = jnp.where(lax.broadcasted_iota(I32, (nr, nr), 1) < lax.broadcasted_iota(I32, (nr, nr), 0),
                      1.0, 0.0).astype(BF16)

    def ranks(mask):
        incl = _dot(mask.astype(BF16), tri)
        before = _dot(below, incl.astype(BF16))[:, LANES - 1:]
        return incl - mask + before, before

    for e in range(N_EXPERTS):
        b = as_bits(aff_ref[e])
        t = thr[e]
        gt = b > t
        eq = b == t
        n_gt = jnp.sum(jnp.sum(jnp.where(gt, 1.0, 0.0), axis=0, keepdims=True), axis=1, keepdims=True)
        eq_rank, _ = ranks(jnp.where(eq, 1.0, 0.0))
        sel = jnp.where(gt, 1.0, jnp.where(eq, jnp.where(eq_rank < cap - n_gt, 1.0, 0.0), 0.0))
        pos, before = ranks(sel)
        pos_ref[e] = jnp.where(sel > 0.5, pos, -1.0).astype(I32)
        rp_ref[e] = before


def _block_geometry(b):
    sb = min(b, ROUTE_SEQS)
    return sb, ROUTE_TILE // sb, b // sb


def _route(aff_t, cap, b):
    n = aff_t.shape[1]
    nr = n // LANES
    sb, tt, groups = _block_geometry(b)
    aff_t = aff_t.reshape(N_EXPERTS, groups, sb, SEQ // tt, tt).transpose(0, 3, 1, 2, 4).reshape(N_EXPERTS, n)
    pos, rp = pl.pallas_call(
        functools.partial(_route_kernel, cap=cap),
        out_shape=[
            jax.ShapeDtypeStruct((N_EXPERTS, nr, LANES), I32),
            jax.ShapeDtypeStruct((N_EXPERTS, nr, 1), F32),
        ],
        compiler_params=pltpu.CompilerParams(vmem_limit_bytes=VMEM_LIMIT),
        name="route",
    )(aff_t.reshape(N_EXPERTS, nr, LANES))
    per_block = ROUTE_TILE // LANES
    offs = rp[:, ::per_block, 0].astype(I32)
    offs = jnp.concatenate([offs, jnp.full((N_EXPERTS, 1), cap, I32)], axis=1)
    return pos.reshape(N_EXPERTS, n), aff_t, offs.reshape(-1)


def _align(v):
    return pl.multiple_of(lax.shift_left(lax.shift_right_logical(v, ROW_ALIGN_SHIFT), ROW_ALIGN_SHIFT), ROW_ALIGN)


def _dispatch_kernel(offs_ref, hn_ref, pos_ref, xe_ref, stage, ostage, carry, sem, osem, *, nb, cap):
    t = pl.program_id(0)
    slot = t % 2
    off = lambda e, tt: offs_ref[e * (nb + 1) + tt]
    rows = lax.broadcasted_iota(I32, (WIN, ROUTE_TILE), 0)
    pos = pos_ref[...]
    hn = hn_ref[...].reshape(ROUTE_TILE, D_MODEL)

    def main_copy(e, tt, sl):
        return pltpu.make_async_copy(stage.at[sl, e * WIN:(e + 1) * WIN],
                                     xe_ref.at[e, pl.ds(_align(off(e, tt)), WIN)], sem.at[e])

    @pl.when(t == 0)
    def _():
        carry[...] = jnp.zeros_like(carry)

    onehot = jnp.concatenate(
        [jnp.where(pos[e:e + 1, :] - _align(off(e, t)) == rows, 1.0, 0.0) for e in range(N_EXPERTS)],
        axis=0).astype(BF16)
    stage[slot] = _dot(onehot, hn).astype(BF16)

    starts = [_align(off(e, t)) for e in range(N_EXPERTS)]
    filled = [off(e, t + 1) - starts[e] for e in range(N_EXPERTS)]
    nwin = [lax.shift_right_logical(f, WIN_SHIFT) + 1 for f in filled]
    tail = [_align(filled[e]) - (nwin[e] - 1) * WIN for e in range(N_EXPERTS)]

    for e in range(N_EXPERTS):
        first = stage[slot, e * WIN:e * WIN + ROW_ALIGN, :]
        stage[slot, e * WIN:e * WIN + ROW_ALIGN, :] = first + carry[e]

    @pl.when(t > 0)
    def _():
        for e in range(N_EXPERTS):
            main_copy(e, t - 1, 1 - slot).wait()

    for e in range(N_EXPERTS):
        main_copy(e, t, slot).start()
        row = e * WIN + jnp.where(nwin[e] == 1, tail[e], 0)
        carry[e] = stage[slot, pl.ds(pl.multiple_of(row, ROW_ALIGN), ROW_ALIGN), :]

    extra_windows = sum(nwin) - N_EXPERTS

    @pl.when(extra_windows > 0)
    def _():
        for e in range(N_EXPERTS):
            def extra(c, unused):
                lo = pl.multiple_of(starts[e] + c * WIN, ROW_ALIGN)
                win_rows = lax.broadcasted_iota(I32, (WIN, ROUTE_TILE), 0)
                oh = jnp.where(pos_ref[e:e + 1, :] - lo == win_rows, 1.0, 0.0).astype(BF16)
                ostage[...] = _dot(oh, hn_ref[...].reshape(ROUTE_TILE, D_MODEL)).astype(BF16)
                cp = pltpu.make_async_copy(ostage, xe_ref.at[e, pl.ds(lo, WIN)], osem)
                cp.start()
                cp.wait()

                @pl.when(c == nwin[e] - 1)
                def _():
                    carry[e] = ostage[pl.ds(pl.multiple_of(tail[e], ROW_ALIGN), ROW_ALIGN), :]
                return unused

            lax.fori_loop(1, nwin[e], extra, 0)

    @pl.when(t == nb - 1)
    def _():
        for e in range(N_EXPERTS):
            main_copy(e, t, slot).wait()
        stage[1 - slot, 0:WIN, :] = jnp.zeros((WIN, D_MODEL), BF16)
        pads = [pltpu.make_async_copy(stage.at[1 - slot, 0:WIN], xe_ref.at[e, cap:cap + WIN], sem.at[e])
                for e in range(N_EXPERTS)]
        for cp in pads:
            cp.start()
        for cp in pads:
            cp.wait()


def _dispatch(hn, pos, offs, cap):
    b = hn.shape[0]
    nb = b * SEQ // ROUTE_TILE
    sb, tt, groups = _block_geometry(b)
    return pl.pallas_call(
        functools.partial(_dispatch_kernel, nb=nb, cap=cap),
        grid_spec=pltpu.PrefetchScalarGridSpec(
            num_scalar_prefetch=1,
            grid=(nb,),
            in_specs=[
                pl.BlockSpec((sb, tt, D_MODEL), lambda t, o: (t % groups, t // groups, 0)),
                pl.BlockSpec((N_EXPERTS, ROUTE_TILE), lambda t, o: (0, t)),
            ],
            out_specs=pl.BlockSpec(memory_space=pl.ANY),
            scratch_shapes=[
                pltpu.VMEM((2, N_EXPERTS * WIN, D_MODEL), BF16),
                pltpu.VMEM((WIN, D_MODEL), BF16),
                pltpu.VMEM((N_EXPERTS, ROW_ALIGN, D_MODEL), BF16),
                pltpu.SemaphoreType.DMA((N_EXPERTS,)),
                pltpu.SemaphoreType.DMA(()),
            ],
        ),
        out_shape=jax.ShapeDtypeStruct((N_EXPERTS, cap + WIN, D_MODEL), BF16),
        compiler_params=_cparams("arbitrary"),
        name="dispatch",
    )(offs, hn, pos)


def _ffn_kernel(xe_ref, wg_ref, wu_ref, wd_ref, ye_ref):
    x = xe_ref[0]
    g = _dot(x, wg_ref[0, 0])
    u = _dot(x, wu_ref[0, 0])
    hid = (g * jax.nn.sigmoid(g) * u).astype(BF16)
    ye_ref[0] = _dot(hid, wd_ref[0, 0]).astype(BF16)


def _ffn(xe, w_gate, w_up, w_down, layer, cap):
    tile = min(FFN_TILE, cap)
    til = pl.BlockSpec((1, tile, D_MODEL), lambda e, j: (e, j, 0))
    wsp = pl.BlockSpec((1, 1, D_MODEL, D_MODEL), lambda e, j: (layer, e, 0, 0))
    return pl.pallas_call(
        _ffn_kernel,
        grid=(N_EXPERTS, cap // tile),
        in_specs=[til, wsp, wsp, wsp],
        out_specs=til,
        out_shape=jax.ShapeDtypeStruct((N_EXPERTS, cap, D_MODEL), BF16),
        compiler_params=_cparams("arbitrary", "arbitrary"),
        name="expert_ffn",
    )(xe, w_gate, w_up, w_down)


def _combine_kernel(offs_ref, x_ref, pos_ref, gate_ref, mod_ref, gfin_ref, ye_ref, o_ref,
                    yin, obuf, yacc, sem, osem, *, nb, cap, final):
    t = pl.program_id(0)
    slot = t % 2
    off = lambda e, tt: offs_ref[e * (nb + 1) + tt]
    base = lambda lo: pl.multiple_of(jnp.minimum(lo, cap - WIN), ROW_ALIGN)
    slots = lax.broadcasted_iota(I32, (WIN, ROUTE_TILE), 0)

    def main_copy(e, tt, sl):
        return pltpu.make_async_copy(ye_ref.at[e, pl.ds(base(_align(off(e, tt))), WIN)],
                                     yin.at[sl, e * WIN:(e + 1) * WIN], sem.at[sl, e])

    @pl.when(t == 0)
    def _():
        for e in range(N_EXPERTS):
            main_copy(e, 0, 0).start()

    ahead = jnp.minimum(t + 1, nb - 1)
    for e in range(N_EXPERTS):
        main_copy(e, ahead, 1 - slot).start()

    pos = pos_ref[...]
    gate = gate_ref[...]

    def weights(e, lo):
        p = pos[e:e + 1, :]
        p = jnp.where(p >= lo, p, -1)
        return jnp.where(p - base(lo) == slots, gate[e:e + 1, :], 0.0).astype(BF16)

    gmat = jnp.concatenate([weights(e, _align(off(e, t))) for e in range(N_EXPERTS)], axis=0)
    for e in range(N_EXPERTS):
        main_copy(e, t, slot).wait()
    yacc[...] = _dot_tn(gmat, yin[slot])

    starts = [_align(off(e, t)) for e in range(N_EXPERTS)]
    nwin = [lax.shift_right_logical(jnp.maximum(off(e, t + 1) - starts[e] - 1, 0), WIN_SHIFT) + 1
            for e in range(N_EXPERTS)]
    extra_windows = sum(nwin) - N_EXPERTS

    @pl.when(extra_windows > 0)
    def _():
        for e in range(N_EXPERTS):
            def extra(c, carry):
                lo = starts[e] + c * WIN
                cp = pltpu.make_async_copy(ye_ref.at[e, pl.ds(base(lo), WIN)], obuf, osem)
                cp.start()
                cp.wait()
                yacc[...] += _dot_tn(weights(e, lo), obuf[...])
                return carry

            lax.fori_loop(1, nwin[e], extra, 0)

    y = x_ref[...] + mod_ref[:, 5:6, :] * yacc[...].reshape(x_ref.shape)
    if final:
        y = _rms(y) * gfin_ref[...]
    o_ref[...] = y

    @pl.when(t == nb - 1)
    def _():
        for e in range(N_EXPERTS):
            main_copy(e, ahead, 1 - slot).wait()


def _combine(x1, pos, gate, mod, g_final, ye, offs, cap, final):
    b = x1.shape[0]
    nb = b * SEQ // ROUTE_TILE
    sb, tt, groups = _block_geometry(b)
    tok = pl.BlockSpec((sb, tt, D_MODEL), lambda t, o: (t % groups, t // groups, 0))
    rt = pl.BlockSpec((N_EXPERTS, ROUTE_TILE), lambda t, o: (0, t))
    return pl.pallas_call(
        functools.partial(_combine_kernel, nb=nb, cap=cap, final=final),
        grid_spec=pltpu.PrefetchScalarGridSpec(
            num_scalar_prefetch=1,
            grid=(nb,),
            in_specs=[
                tok, rt, rt,
                pl.BlockSpec((sb, 6, D_MODEL), lambda t, o: (t % groups, 0, 0)),
                pl.BlockSpec((1, D_MODEL), lambda t, o: (0, 0)),
                pl.BlockSpec(memory_space=pl.ANY),
            ],
            out_specs=tok,
            scratch_shapes=[
                pltpu.VMEM((2, N_EXPERTS * WIN, D_MODEL), BF16),
                pltpu.VMEM((WIN, D_MODEL), BF16),
                pltpu.VMEM((ROUTE_TILE, D_MODEL), F32),
                pltpu.SemaphoreType.DMA((2, N_EXPERTS)),
                pltpu.SemaphoreType.DMA(()),
            ],
        ),
        out_shape=jax.ShapeDtypeStruct((b, SEQ, D_MODEL), F32),
        compiler_params=_cparams("arbitrary"),
        name="combine",
    )(offs, x1, pos, gate, mod, g_final.reshape(1, D_MODEL), ye)


def _trunk(x, c, w_ada, b_ada, g_mix, w_in, sink, q_gain, k_gain, nbr_bias, g_grp, w_out,
           g_ffn, w_router, w_gate, w_up, w_down, g_final):
    b = x.shape[0]
    n = b * SEQ
    cap = CAPACITY_FACTOR * n // N_EXPERTS
    depth = w_ada.shape[0]
    for i in range(depth):
        mod = _ada(c, w_ada, b_ada[i], i).reshape(b, 6, D_MODEL)
        qa, ka, va, qb, kb, vb, qc, kc, vc = _inproj(x, mod, g_mix[i], w_in, i, q_gain[i], k_gain[i])
        ma = _attn_a(qa, ka, va, sink[i], g_grp[i, :A_W])
        mb = _attn_b(qb, kb, vb, g_grp[i, A_W:A_W + B_W])
        mc = _attn_c(qc, kc, vc, nbr_bias[i], g_grp[i, A_W + B_W:])
        x1, hn, aff = _outproj(x, ma, mb, mc, mod, w_out, i, g_ffn[i], w_router[i])
        pos, gate, offs = _route(aff, cap, b)
        xe = _dispatch(hn.reshape(b, SEQ, D_MODEL), pos, offs, cap)
        ye = _ffn(xe, w_gate, w_up, w_down, i, cap)
        x = _combine(x1, pos, gate, mod, g_final, ye, offs, cap, final=(i == depth - 1))
    return x


def kernel(x_prompt, x_sample, c_prompt, c_sample, w_ada, b_ada, g_mix, w_in, sink, q_gain, k_gain, rpb, g_grp,
           w_out, g_ffn, w_router, w_gate, w_up, w_down, g_final):
    w_in, w_out, w_gate, w_up, w_down = (w.astype(BF16) for w in (w_in, w_out, w_gate, w_up, w_down))
    nbr_bias = [_nbr_bias(rpb[i]) for i in range(rpb.shape[0])]
    args = (w_ada, b_ada, g_mix, w_in, sink, q_gain, k_gain, nbr_bias, g_grp, w_out,
            g_ffn, w_router, w_gate, w_up, w_down, g_final)
    return (_trunk(x_prompt, c_prompt, *args), _trunk(x_sample, c_sample, *args))
```

```python
import functools

import numpy as np
import jax
import jax.numpy as jnp
from jax import lax
from jax.experimental import pallas as pl
from jax.experimental.pallas import tpu as pltpu

F32 = jnp.float32
BF16 = jnp.bfloat16
I32 = jnp.int32

D_MODEL = 1024
SEQ = 2048
HEAD_DIM = 64
A_HEADS, A_KV = 8, 2
B_HEADS, B_KV = 4, 2
C_HEADS = 4
A_W, B_W, C_W = A_HEADS * HEAD_DIM, B_HEADS * HEAD_DIM, C_HEADS * HEAD_DIM
IN_WIDTH = 2048
WINDOW = 128
GRID_W = 64
GRID_ROWS = SEQ // GRID_W
NA_ROWS, NA_COLS = 8, 16
N_EXPERTS = 16
CAPACITY_FACTOR = 2
ROPE_THETA = 10000.0
EPS = 1e-6
NEG_INF = -1e30
Q_SCALE = HEAD_DIM ** -0.5

LANES = 128
ROW_ALIGN = 16
ROW_ALIGN_SHIFT = 4
TOK_TILE = 512
MIX_TILE = 256
NBR_TILE_ROWS = 4
NBR_KEY_ROWS = NA_ROWS + NBR_TILE_ROWS - 1
ROUTE_TILE = 256
ROUTE_SEQS = 16
WIN = 64
WIN_SHIFT = 6
FFN_TILE = 1024
VMEM_LIMIT = 56 * 1024 * 1024


def _cparams(*sem):
    return pltpu.CompilerParams(dimension_semantics=sem, vmem_limit_bytes=VMEM_LIMIT)


def _split_bf16(a):
    hi = a.astype(BF16)
    lo = (a - hi.astype(F32)).astype(BF16)
    return hi, lo


def _dot(a, b):
    return jnp.dot(a, b, preferred_element_type=F32)


def _dot_nt(a, b):
    return lax.dot_general(a, b, (((1,), (1,)), ((), ())), preferred_element_type=F32)


def _dot_tn(a, b):
    return lax.dot_general(a, b, (((0,), (0,)), ((), ())), preferred_element_type=F32)


def _dot3(a, w):
    ah, al = _split_bf16(a)
    wh, wl = _split_bf16(w)
    return _dot(ah, wh) + _dot(al, wh) + _dot(ah, wl)


def _rms(x):
    return x * lax.rsqrt(jnp.mean(x * x, axis=-1, keepdims=True) + EPS)


def _ada_kernel(c_ref, w_ref, b_ref, o_ref):
    c = c_ref[...]
    a = c * jax.nn.sigmoid(c)
    o_ref[...] = _dot3(a, w_ref[0]) + b_ref[...]


def _ada(c, w_ada, b_ada, layer):
    b = c.shape[0]
    return pl.pallas_call(
        _ada_kernel,
        grid=(6,),
        in_specs=[
            pl.BlockSpec((b, D_MODEL), lambda j: (0, 0)),
            pl.BlockSpec((1, D_MODEL, D_MODEL), lambda j: (layer, 0, j)),
            pl.BlockSpec((1, D_MODEL), lambda j: (0, j)),
        ],
        out_specs=pl.BlockSpec((b, D_MODEL), lambda j: (0, j)),
        out_shape=jax.ShapeDtypeStruct((b, 6 * D_MODEL), F32),
        compiler_params=_cparams("arbitrary"),
        name="ada",
    )(c, w_ada, b_ada.reshape(1, 6 * D_MODEL))


def _rotate(x, cos, sin_signed, half):
    w = x.shape[1]
    lane = lax.broadcasted_iota(I32, x.shape, 1)
    first = (lane % (2 * half)) < half
    rot = jnp.where(first, pltpu.roll(x, w - half, 1), pltpu.roll(x, half, 1))
    return x * cos + rot * sin_signed


def _head_rms(x, hm, gain):
    sq = x * x
    sh, sl = _split_bf16(sq)
    ms = (_dot(sh, hm) + _dot(sl, hm)) * (1.0 / HEAD_DIM)
    return x * lax.rsqrt(ms + EPS) * gain


def _store_values(v_ref, v):
    low = lax.broadcasted_iota(I32, (v.shape[0], LANES), 1) < HEAD_DIM
    for h in range(v.shape[1] // HEAD_DIM):
        pair = v[:, (h // 2) * LANES:(h // 2 + 1) * LANES]
        vals = pair if h % 2 == 0 else pltpu.roll(pair, HEAD_DIM, 1)
        v_ref[0, :, h * LANES:(h + 1) * LANES] = jnp.where(low, vals, 1.0).astype(BF16)


def _inproj_kernel(x_ref, mod_ref, g_ref, w_ref, ca_ref, sa_ref, cb_ref, sb_ref, qg_ref, kg_ref, hm_ref,
                   qa_ref, ka_ref, va_ref, qb_ref, kb_ref, vb_ref, qc_ref, kc_ref, vc_ref):
    x = x_ref[0]
    hn = _rms(x) * g_ref[...]
    hn = hn * (1.0 + mod_ref[0, 1:2, :]) + mod_ref[0, 0:1, :]
    proj = _dot(hn.astype(BF16), w_ref[0])
    ca, sa, cb, sb = ca_ref[...], sa_ref[...], cb_ref[...], sb_ref[...]
    hm = hm_ref[...]

    col = 0
    for j in range(A_W // LANES):
        qa_ref[0, :, j * LANES:(j + 1) * LANES] = (
            _rotate(proj[:, col:col + LANES], ca, sa, HEAD_DIM // 2) * Q_SCALE).astype(BF16)
        col += LANES
    ka_ref[0] = _rotate(proj[:, col:col + LANES], ca, sa, HEAD_DIM // 2).astype(BF16)
    col += LANES
    _store_values(va_ref, proj[:, col:col + LANES])
    col += LANES
    for j in range(B_W // LANES):
        q = _head_rms(proj[:, col:col + LANES], hm, qg_ref[...])
        qb_ref[0, :, j * LANES:(j + 1) * LANES] = (_rotate(q, cb, sb, HEAD_DIM // 4) * Q_SCALE).astype(BF16)
        col += LANES
    k = _head_rms(proj[:, col:col + LANES], hm, kg_ref[...])
    kb_ref[0] = _rotate(k, cb, sb, HEAD_DIM // 4).astype(BF16)
    col += LANES
    _store_values(vb_ref, proj[:, col:col + LANES])
    col += LANES
    qc_ref[0] = (proj[:, col:col + C_W] * Q_SCALE).astype(BF16)
    col += C_W
    kc_ref[0] = proj[:, col:col + C_W].astype(BF16)
    col += C_W
    _store_values(vc_ref, proj[:, col:col + C_W])


def _rope_tables():
    t = np.arange(SEQ, dtype=np.float64)[:, None]
    j = np.arange(LANES)[None, :] % HEAD_DIM
    inv_a = ROPE_THETA ** (-(2.0 * (j % 32)) / HEAD_DIM)
    ang_a = t * inv_a
    sign_a = np.where(j < 32, -1.0, 1.0)
    inv_b = ROPE_THETA ** (-(2.0 * (j % 16)) / (HEAD_DIM // 2))
    pos_b = np.where(j < 32, np.floor(t / GRID_W), t % GRID_W)
    ang_b = pos_b * inv_b
    sign_b = np.where((j % 32) < 16, -1.0, 1.0)
    f = lambda a: jnp.asarray(a, dtype=F32)
    return f(np.cos(ang_a)), f(np.sin(ang_a) * sign_a), f(np.cos(ang_b)), f(np.sin(ang_b) * sign_b)


def _inproj(x, mod, g_mix, w_in, layer, q_gain, k_gain):
    b = x.shape[0]
    ca, sa, cb, sb = _rope_tables()
    hm = jnp.asarray(np.kron(np.eye(LANES // HEAD_DIM), np.ones((HEAD_DIM, HEAD_DIM))), dtype=BF16)
    gain2 = lambda g: jnp.tile(g, LANES // HEAD_DIM).reshape(1, LANES)
    tok = lambda w: pl.BlockSpec((1, TOK_TILE, w), lambda i, j: (i, j, 0))
    tab = pl.BlockSpec((TOK_TILE, LANES), lambda i, j: (j, 0))
    full = lambda r, c: pl.BlockSpec((r, c), lambda i, j: (0, 0))
    widths = (A_W, LANES, A_KV * LANES, B_W, LANES, B_KV * LANES, C_W, C_W, C_HEADS * LANES)
    return pl.pallas_call(
        _inproj_kernel,
        grid=(b, SEQ // TOK_TILE),
        in_specs=[
            tok(D_MODEL),
            pl.BlockSpec((1, 6, D_MODEL), lambda i, j: (i, 0, 0)),
            full(1, D_MODEL),
            pl.BlockSpec((1, D_MODEL, IN_WIDTH), lambda i, j: (layer, 0, 0)),
            tab, tab, tab, tab,
            full(1, LANES), full(1, LANES), full(LANES, LANES),
        ],
        out_specs=[tok(w) for w in widths],
        out_shape=[jax.ShapeDtypeStruct((b, SEQ, w), BF16) for w in widths],
        compiler_params=_cparams("arbitrary", "arbitrary"),
        name="inproj",
    )(x, mod, g_mix.reshape(1, D_MODEL), w_in, ca, sa, cb, sb, gain2(q_gain), gain2(k_gain), hm)


def _softmax_pv_aug(s, v_aug, extra=None):
    m = jnp.max(s, axis=-1, keepdims=True)
    if extra is not None:
        m = jnp.maximum(m, extra)
    p = jnp.exp((s - m).astype(BF16))
    o = _dot(p, v_aug)
    l = o[:, HEAD_DIM:HEAD_DIM + 1]
    if extra is not None:
        l = l + jnp.exp(extra - m)
    return o[:, 0:HEAD_DIM] * (1.0 / l)


def _window_masks():
    rows = (A_HEADS // A_KV) * WINDOW
    qi = np.arange(rows)[:, None] % WINDOW
    ki = np.arange(3 * WINDOW)[None, :]
    masks = [np.where(np.abs(ki - qi - shift * WINDOW) <= WINDOW, 0.0, NEG_INF) for shift in range(3)]
    return jnp.asarray(np.stack(masks), dtype=F32)


def _nbr_bias(rpb):
    edge = GRID_W - NA_COLS
    padded = jnp.pad(rpb, ((0, 0), (0, 0), (edge, edge)))
    toe = jnp.stack([padded[..., GRID_W - 1 - c:2 * GRID_W - 1 - c] for c in range(GRID_W)], axis=2)
    c = np.arange(GRID_W)[:, None]
    kc = np.arange(GRID_W)[None, :]
    cs = np.clip(c - NA_COLS // 2, 0, GRID_W - NA_COLS)
    toe = jnp.where((kc >= cs) & (kc < cs + NA_COLS), toe, NEG_INF)
    masked = jnp.full((C_HEADS, GRID_W, GRID_W), NEG_INF, F32)
    ntiles = GRID_ROWS // NBR_TILE_ROWS
    classes = []
    for r0 in (0, NBR_TILE_ROWS, (ntiles - 1) * NBR_TILE_ROWS):
        ws = min(max(r0 - NA_ROWS // 2, 0), GRID_ROWS - NBR_KEY_ROWS)
        per_query_row = []
        for rq in range(NBR_TILE_ROWS):
            r = r0 + rq
            rs = min(max(r - NA_ROWS // 2, 0), GRID_ROWS - NA_ROWS)
            blocks = [toe[:, ws + kr - r + NA_ROWS - 1] if rs <= ws + kr < rs + NA_ROWS else masked
                      for kr in range(NBR_KEY_ROWS)]
            per_query_row.append(jnp.stack(blocks, axis=2))
        classes.append(jnp.stack(per_query_row, axis=1))
    return jnp.stack(classes).reshape(3, C_HEADS, NBR_TILE_ROWS * GRID_W, NBR_KEY_ROWS * GRID_W)


def _mixers_kernel(sink_ref, qa_ref, ka_ref, va_ref, qb_ref, kb_ref, vb_ref, qc_ref, kc_ref, vc_ref,
                   g_ref, mask_ref, bias_ref, o_ref):
    j = pl.program_id(1)
    span = WINDOW * 3
    group_a = A_HEADS // A_KV
    rows_a = group_a * WINDOW
    nblocks = SEQ // WINDOW
    head_of_row = lax.broadcasted_iota(I32, (rows_a, 1), 0) // WINDOW
    sinks = []
    for kv in range(A_KV):
        sink = jnp.zeros((rows_a, 1), F32)
        for h in range(group_a):
            sink = jnp.where(head_of_row == h, sink_ref[kv * group_a + h], sink)
        sinks.append(sink)
    outs_a, outs_b, outs_c = {}, {}, {}

    def window_task(blk, kv):
        i = j * (MIX_TILE // WINDOW) + blk
        kstart = pl.multiple_of(jnp.clip(i * WINDOW - WINDOW, 0, SEQ - span), WINDOW)
        case = jnp.where(i == 0, 0, jnp.where(i == nblocks - 1, 2, 1))

        def score():
            qs = jnp.concatenate([qa_ref[0, blk * WINDOW:(blk + 1) * WINDOW, h * HEAD_DIM:(h + 1) * HEAD_DIM]
                                  for h in range(kv * group_a, (kv + 1) * group_a)], axis=0)
            return _dot_nt(qs, ka_ref[0, pl.ds(kstart, span), kv * HEAD_DIM:(kv + 1) * HEAD_DIM]) + mask_ref[case]

        def finish(s):
            o = _softmax_pv_aug(s, va_ref[0, pl.ds(kstart, span), kv * LANES:(kv + 1) * LANES], extra=sinks[kv])
            outs_a[blk, kv] = [o[h * WINDOW:(h + 1) * WINDOW] for h in range(group_a)]
        return score, finish

    def dense_task(h):
        kv = h // (B_HEADS // B_KV)

        def score():
            return _dot_nt(qb_ref[0, :, h * HEAD_DIM:(h + 1) * HEAD_DIM],
                           kb_ref[0, :, kv * HEAD_DIM:(kv + 1) * HEAD_DIM])

        def finish(s):
            outs_b[h] = _softmax_pv_aug(s, vb_ref[0, :, kv * LANES:(kv + 1) * LANES])
        return score, finish

    ntiles = GRID_ROWS // NBR_TILE_ROWS
    nkeys = NBR_KEY_ROWS * GRID_W
    ws = jnp.clip(j * NBR_TILE_ROWS - NA_ROWS // 2, 0, GRID_ROWS - NBR_KEY_ROWS)
    cls = jnp.where(j == 0, 0, jnp.where(j == ntiles - 1, 2, 1))
    kstart_c = pl.multiple_of(ws * GRID_W, GRID_W)

    def nbr_task(h):
        cols = slice(h * HEAD_DIM, (h + 1) * HEAD_DIM)

        def score():
            return _dot_nt(qc_ref[0, :, cols], kc_ref[0, pl.ds(kstart_c, nkeys), cols]) + bias_ref[cls, h]

        def finish(s):
            outs_c[h] = _softmax_pv_aug(s, vc_ref[0, pl.ds(kstart_c, nkeys), h * LANES:(h + 1) * LANES])
        return score, finish

    window = [window_task(blk, kv) for blk in range(MIX_TILE // WINDOW) for kv in range(A_KV)]
    tasks = []
    for t in range(4):
        tasks += [dense_task(t), window[t], nbr_task(t)]
    scores = tasks[0][0]()
    for t, (_, finish) in enumerate(tasks):
        cur, scores = scores, (tasks[t + 1][0]() if t + 1 < len(tasks) else None)
        finish(cur)

    g = g_ref[...]
    for blk in range(MIX_TILE // WINDOW):
        o = jnp.concatenate(outs_a[blk, 0] + outs_a[blk, 1], axis=1)
        o_ref[0, blk * WINDOW:(blk + 1) * WINDOW, 0:A_W] = (_rms(o) * g[:, 0:A_W]).astype(BF16)
    o = jnp.concatenate([outs_b[h] for h in range(B_HEADS)], axis=1)
    o_ref[0, :, A_W:A_W + B_W] = (_rms(o) * g[:, A_W:A_W + B_W]).astype(BF16)
    o = jnp.concatenate([outs_c[h] for h in range(C_HEADS)], axis=1)
    o_ref[0, :, A_W + B_W:] = (_rms(o) * g[:, A_W + B_W:]).astype(BF16)


def _mixers(qa, ka, va, qb, kb, vb, qc, kc, vc, sink, bias, g_grp):
    b = qa.shape[0]
    til = lambda w: pl.BlockSpec((1, MIX_TILE, w), lambda i, j: (i, j, 0))
    seq = lambda w: pl.BlockSpec((1, SEQ, w), lambda i, j: (i, 0, 0))
    const = lambda a: pl.BlockSpec(a.shape, lambda i, j: (0,) * a.ndim)
    masks = _window_masks()
    width = A_W + B_W + C_W
    return pl.pallas_call(
        _mixers_kernel,
        grid=(b, SEQ // MIX_TILE),
        in_specs=[
            pl.BlockSpec(memory_space=pltpu.SMEM),
            til(A_W), seq(LANES), seq(A_KV * LANES),
            til(B_W), seq(LANES), seq(B_KV * LANES),
            til(C_W), seq(C_W), seq(C_HEADS * LANES),
            pl.BlockSpec((1, width), lambda i, j: (0, 0)),
            const(masks), const(bias),
        ],
        out_specs=til(width),
        out_shape=jax.ShapeDtypeStruct((b, SEQ, width), BF16),
        compiler_params=_cparams("arbitrary", "arbitrary"),
        name="mixers",
    )(sink, qa, ka, va, qb, kb, vb, qc, kc, vc, g_grp.reshape(1, width), masks, bias)


def _outproj_kernel(x_ref, m_ref, mod_ref, wo_ref, g_ref, wr_ref, x1_ref, hn_ref, aff_ref):
    parts = 2
    rows = x_ref.shape[1] // parts
    mixed = lambda p: _dot(m_ref[0, p * rows:(p + 1) * rows, :], wo_ref[0])

    nxt = mixed(0)
    for p in range(parts):
        r = slice(p * rows, (p + 1) * rows)
        attn, nxt = nxt, (mixed(p + 1) if p + 1 < parts else None)
        x1 = x_ref[0, r, :] + mod_ref[0, 2:3, :] * attn
        x1_ref[0, r, :] = x1
        hn = _rms(x1) * g_ref[...]
        hn = hn * (1.0 + mod_ref[0, 4:5, :]) + mod_ref[0, 3:4, :]
        hn_ref[r, :] = hn.astype(BF16)
        logits = _dot3(hn, wr_ref[...])
        lt = logits.T[0:N_EXPERTS, :]
        e = jnp.exp(lt - jnp.max(lt, axis=0, keepdims=True))
        aff_ref[:, r] = e * (1.0 / jnp.sum(e, axis=0, keepdims=True))


def _outproj(x, merged, mod, w_out, layer, g_ffn, w_router):
    b = x.shape[0]
    n = b * SEQ
    nt = SEQ // TOK_TILE
    tok = lambda w: pl.BlockSpec((1, TOK_TILE, w), lambda i, j: (i, j, 0))
    wr = jnp.pad(w_router, ((0, 0), (0, LANES - N_EXPERTS)))
    return pl.pallas_call(
        _outproj_kernel,
        grid=(b, nt),
        in_specs=[
            tok(D_MODEL), tok(A_W + B_W + C_W),
            pl.BlockSpec((1, 6, D_MODEL), lambda i, j: (i, 0, 0)),
            pl.BlockSpec((1, D_MODEL, D_MODEL), lambda i, j: (layer, 0, 0)),
            pl.BlockSpec((1, D_MODEL), lambda i, j: (0, 0)),
            pl.BlockSpec((D_MODEL, LANES), lambda i, j: (0, 0)),
        ],
        out_specs=[
            tok(D_MODEL),
            pl.BlockSpec((TOK_TILE, D_MODEL), lambda i, j: (i * nt + j, 0)),
            pl.BlockSpec((N_EXPERTS, TOK_TILE), lambda i, j: (0, i * nt + j)),
        ],
        out_shape=[
            jax.ShapeDtypeStruct((b, SEQ, D_MODEL), F32),
            jax.ShapeDtypeStruct((n, D_MODEL), BF16),
            jax.ShapeDtypeStruct((N_EXPERTS, n), F32),
        ],
        compiler_params=_cparams("arbitrary", "arbitrary"),
        name="outproj_router",
    )(x, merged, mod, w_out, g_ffn.reshape(1, D_MODEL), wr)


def _route_kernel(aff_ref, pos_ref, rp_ref, *, cap):
    nr = aff_ref.shape[1]
    as_bits = lambda a: lax.bitcast_convert_type(a, I32)

    def bisect(_, carry):
        lo, hi = carry
        mid = lo + ((hi - lo) >> 1)
        ge = jnp.where(as_bits(aff_ref[...]) >= mid, 1.0, 0.0)
        cnt = jnp.sum(jnp.sum(ge, axis=1, keepdims=True), axis=2, keepdims=True)
        keep = cnt >= cap
        return jnp.where(keep, mid, lo), jnp.where(keep, hi, mid)

    lo0 = jnp.zeros((N_EXPERTS, 1, 1), I32)
    hi0 = jnp.full((N_EXPERTS, 1, 1), 0x7F800000, I32)
    thr, _ = lax.fori_loop(0, 31, bisect, (lo0, hi0))

    tri = jnp.where(lax.broadcasted_iota(I32, (LANES, LANES), 0) <= lax.broadcasted_iota(I32, (LANES, LANES), 1),
                    1.0, 0.0).astype(BF16)
    below = jnp.where(lax.broadcasted_iota(I32, (nr, nr), 1) < lax.broadcasted_iota(I32, (nr, nr), 0),
                      1.0, 0.0).astype(BF16)

    def ranks(mask):
        incl = _dot(mask.astype(BF16), tri)
        before = _dot(below, incl.astype(BF16))[:, LANES - 1:]
        return incl - mask + before, before

    for e in range(N_EXPERTS):
        b = as_bits(aff_ref[e])
        t = thr[e]
        gt = b > t
        eq = b == t
        n_gt = jnp.sum(jnp.sum(jnp.where(gt, 1.0, 0.0), axis=0, keepdims=True), axis=1, keepdims=True)
        eq_rank, _ = ranks(jnp.where(eq, 1.0, 0.0))
        sel = jnp.where(gt, 1.0, jnp.where(eq, jnp.where(eq_rank < cap - n_gt, 1.0, 0.0), 0.0))
        pos, before = ranks(sel)
        pos_ref[e] = jnp.where(sel > 0.5, pos, -1.0).astype(I32)
        rp_ref[e] = before


def _block_geometry(b):
    sb = min(b, ROUTE_SEQS)
    return sb, ROUTE_TILE // sb, b // sb


def _route(aff_t, cap, b):
    n = aff_t.shape[1]
    nr = n // LANES
    sb, tt, groups = _block_geometry(b)
    aff_t = aff_t.reshape(N_EXPERTS, groups, sb, SEQ // tt, tt).transpose(0, 3, 1, 2, 4).reshape(N_EXPERTS, n)
    pos, rp = pl.pallas_call(
        functools.partial(_route_kernel, cap=cap),
        out_shape=[
            jax.ShapeDtypeStruct((N_EXPERTS, nr, LANES), I32),
            jax.ShapeDtypeStruct((N_EXPERTS, nr, 1), F32),
        ],
        compiler_params=pltpu.CompilerParams(vmem_limit_bytes=VMEM_LIMIT),
        name="route",
    )(aff_t.reshape(N_EXPERTS, nr, LANES))
    per_block = ROUTE_TILE // LANES
    offs = rp[:, ::per_block, 0].astype(I32)
    offs = jnp.concatenate([offs, jnp.full((N_EXPERTS, 1), cap, I32)], axis=1)
    return pos.reshape(N_EXPERTS, n), aff_t, offs.reshape(-1)


def _align(v):
    return pl.multiple_of(lax.shift_left(lax.shift_right_logical(v, ROW_ALIGN_SHIFT), ROW_ALIGN_SHIFT), ROW_ALIGN)


def _dispatch_kernel(offs_ref, hn_ref, pos_ref, xe_ref, stage, ostage, carry, sem, osem, *, nb, cap):
    t = pl.program_id(0)
    slot = t % 2
    off = lambda e, tt: offs_ref[e * (nb + 1) + tt]
    rows = lax.broadcasted_iota(I32, (WIN, ROUTE_TILE), 0)
    pos = pos_ref[...]
    hn = hn_ref[...].reshape(ROUTE_TILE, D_MODEL)

    def main_copy(e, tt, sl):
        return pltpu.make_async_copy(stage.at[sl, e * WIN:(e + 1) * WIN],
                                     xe_ref.at[e, pl.ds(_align(off(e, tt)), WIN)], sem.at[e])

    @pl.when(t == 0)
    def _():
        carry[...] = jnp.zeros_like(carry)

    onehot = jnp.concatenate(
        [jnp.where(pos[e:e + 1, :] - _align(off(e, t)) == rows, 1.0, 0.0) for e in range(N_EXPERTS)],
        axis=0).astype(BF16)
    stage[slot] = _dot(onehot, hn).astype(BF16)

    starts = [_align(off(e, t)) for e in range(N_EXPERTS)]
    filled = [off(e, t + 1) - starts[e] for e in range(N_EXPERTS)]
    nwin = [lax.shift_right_logical(f, WIN_SHIFT) + 1 for f in filled]
    tail = [_align(filled[e]) - (nwin[e] - 1) * WIN for e in range(N_EXPERTS)]

    for e in range(N_EXPERTS):
        first = stage[slot, e * WIN:e * WIN + ROW_ALIGN, :]
        stage[slot, e * WIN:e * WIN + ROW_ALIGN, :] = first + carry[e]

    @pl.when(t > 0)
    def _():
        for e in range(N_EXPERTS):
            main_copy(e, t - 1, 1 - slot).wait()

    for e in range(N_EXPERTS):
        main_copy(e, t, slot).start()
        row = e * WIN + jnp.where(nwin[e] == 1, tail[e], 0)
        carry[e] = stage[slot, pl.ds(pl.multiple_of(row, ROW_ALIGN), ROW_ALIGN), :]

    extra_windows = sum(nwin) - N_EXPERTS

    @pl.when(extra_windows > 0)
    def _():
        for e in range(N_EXPERTS):
            def extra(c, unused):
                lo = pl.multiple_of(starts[e] + c * WIN, ROW_ALIGN)
                win_rows = lax.broadcasted_iota(I32, (WIN, ROUTE_TILE), 0)
                oh = jnp.where(pos_ref[e:e + 1, :] - lo == win_rows, 1.0, 0.0).astype(BF16)
                ostage[...] = _dot(oh, hn_ref[...].reshape(ROUTE_TILE, D_MODEL)).astype(BF16)
                cp = pltpu.make_async_copy(ostage, xe_ref.at[e, pl.ds(lo, WIN)], osem)
                cp.start()
                cp.wait()

                @pl.when(c == nwin[e] - 1)
                def _():
                    carry[e] = ostage[pl.ds(pl.multiple_of(tail[e], ROW_ALIGN), ROW_ALIGN), :]
                return unused

            lax.fori_loop(1, nwin[e], extra, 0)

    @pl.when(t == nb - 1)
    def _():
        for e in range(N_EXPERTS):
            main_copy(e, t, slot).wait()
        stage[1 - slot, 0:WIN, :] = jnp.zeros((WIN, D_MODEL), BF16)
        pads = [pltpu.make_async_copy(stage.at[1 - slot, 0:WIN], xe_ref.at[e, cap:cap + WIN], sem.at[e])
                for e in range(N_EXPERTS)]
        for cp in pads:
            cp.start()
        for cp in pads:
            cp.wait()


def _dispatch(hn, pos, offs, cap):
    b = hn.shape[0]
    nb = b * SEQ // ROUTE_TILE
    sb, tt, groups = _block_geometry(b)
    return pl.pallas_call(
        functools.partial(_dispatch_kernel, nb=nb, cap=cap),
        grid_spec=pltpu.PrefetchScalarGridSpec(
            num_scalar_prefetch=1,
            grid=(nb,),
            in_specs=[
                pl.BlockSpec((sb, tt, D_MODEL), lambda t, o: (t % groups, t // groups, 0)),
                pl.BlockSpec((N_EXPERTS, ROUTE_TILE), lambda t, o: (0, t)),
            ],
            out_specs=pl.BlockSpec(memory_space=pl.ANY),
            scratch_shapes=[
                pltpu.VMEM((2, N_EXPERTS * WIN, D_MODEL), BF16),
                pltpu.VMEM((WIN, D_MODEL), BF16),
                pltpu.VMEM((N_EXPERTS, ROW_ALIGN, D_MODEL), BF16),
                pltpu.SemaphoreType.DMA((N_EXPERTS,)),
                pltpu.SemaphoreType.DMA(()),
            ],
        ),
        out_shape=jax.ShapeDtypeStruct((N_EXPERTS, cap + WIN, D_MODEL), BF16),
        compiler_params=_cparams("arbitrary"),
        name="dispatch",
    )(offs, hn, pos)


def _ffn_kernel(xe_ref, wg_ref, wu_ref, wd_ref, ye_ref):
    x = xe_ref[0]
    g = _dot(x, wg_ref[0, 0])
    u = _dot(x, wu_ref[0, 0])
    hid = (g * jax.nn.sigmoid(g) * u).astype(BF16)
    ye_ref[0] = _dot(hid, wd_ref[0, 0]).astype(BF16)


def _ffn(xe, w_gate, w_up, w_down, layer, cap):
    tile = min(FFN_TILE, cap)
    til = pl.BlockSpec((1, tile, D_MODEL), lambda e, j: (e, j, 0))
    wsp = pl.BlockSpec((1, 1, D_MODEL, D_MODEL), lambda e, j: (layer, e, 0, 0))
    return pl.pallas_call(
        _ffn_kernel,
        grid=(N_EXPERTS, cap // tile),
        in_specs=[til, wsp, wsp, wsp],
        out_specs=til,
        out_shape=jax.ShapeDtypeStruct((N_EXPERTS, cap, D_MODEL), BF16),
        compiler_params=_cparams("arbitrary", "arbitrary"),
        name="expert_ffn",
    )(xe, w_gate, w_up, w_down)


def _combine_kernel(offs_ref, x_ref, pos_ref, gate_ref, mod_ref, gfin_ref, ye_ref, o_ref,
                    yin, obuf, yacc, sem, osem, *, nb, cap, final):
    t = pl.program_id(0)
    slot = t % 2
    off = lambda e, tt: offs_ref[e * (nb + 1) + tt]
    base = lambda lo: pl.multiple_of(jnp.minimum(lo, cap - WIN), ROW_ALIGN)
    slots = lax.broadcasted_iota(I32, (WIN, ROUTE_TILE), 0)

    def main_copy(e, tt, sl):
        return pltpu.make_async_copy(ye_ref.at[e, pl.ds(base(_align(off(e, tt))), WIN)],
                                     yin.at[sl, e * WIN:(e + 1) * WIN], sem.at[sl, e])

    @pl.when(t == 0)
    def _():
        for e in range(N_EXPERTS):
            main_copy(e, 0, 0).start()

    ahead = jnp.minimum(t + 1, nb - 1)
    for e in range(N_EXPERTS):
        main_copy(e, ahead, 1 - slot).start()

    pos = pos_ref[...]
    gate = gate_ref[...]

    def weights(e, lo):
        p = pos[e:e + 1, :]
        p = jnp.where(p >= lo, p, -1)
        return jnp.where(p - base(lo) == slots, gate[e:e + 1, :], 0.0).astype(BF16)

    gmat = jnp.concatenate([weights(e, _align(off(e, t))) for e in range(N_EXPERTS)], axis=0)
    for e in range(N_EXPERTS):
        main_copy(e, t, slot).wait()
    yacc[...] = _dot_tn(gmat, yin[slot])

    starts = [_align(off(e, t)) for e in range(N_EXPERTS)]
    nwin = [lax.shift_right_logical(jnp.maximum(off(e, t + 1) - starts[e] - 1, 0), WIN_SHIFT) + 1
            for e in range(N_EXPERTS)]
    extra_windows = sum(nwin) - N_EXPERTS

    @pl.when(extra_windows > 0)
    def _():
        for e in range(N_EXPERTS):
            def extra(c, carry):
                lo = starts[e] + c * WIN
                cp = pltpu.make_async_copy(ye_ref.at[e, pl.ds(base(lo), WIN)], obuf, osem)
                cp.start()
                cp.wait()
                yacc[...] += _dot_tn(weights(e, lo), obuf[...])
                return carry

            lax.fori_loop(1, nwin[e], extra, 0)

    y = x_ref[...] + mod_ref[:, 5:6, :] * yacc[...].reshape(x_ref.shape)
    if final:
        y = _rms(y) * gfin_ref[...]
    o_ref[...] = y

    @pl.when(t == nb - 1)
    def _():
        for e in range(N_EXPERTS):
            main_copy(e, ahead, 1 - slot).wait()


def _combine(x1, pos, gate, mod, g_final, ye, offs, cap, final):
    b = x1.shape[0]
    nb = b * SEQ // ROUTE_TILE
    sb, tt, groups = _block_geometry(b)
    tok = pl.BlockSpec((sb, tt, D_MODEL), lambda t, o: (t % groups, t // groups, 0))
    rt = pl.BlockSpec((N_EXPERTS, ROUTE_TILE), lambda t, o: (0, t))
    return pl.pallas_call(
        functools.partial(_combine_kernel, nb=nb, cap=cap, final=final),
        grid_spec=pltpu.PrefetchScalarGridSpec(
            num_scalar_prefetch=1,
            grid=(nb,),
            in_specs=[
                tok, rt, rt,
                pl.BlockSpec((sb, 6, D_MODEL), lambda t, o: (t % groups, 0, 0)),
                pl.BlockSpec((1, D_MODEL), lambda t, o: (0, 0)),
                pl.BlockSpec(memory_space=pl.ANY),
            ],
            out_specs=tok,
            scratch_shapes=[
                pltpu.VMEM((2, N_EXPERTS * WIN, D_MODEL), BF16),
                pltpu.VMEM((WIN, D_MODEL), BF16),
                pltpu.VMEM((ROUTE_TILE, D_MODEL), F32),
                pltpu.SemaphoreType.DMA((2, N_EXPERTS)),
                pltpu.SemaphoreType.DMA(()),
            ],
        ),
        out_shape=jax.ShapeDtypeStruct((b, SEQ, D_MODEL), F32),
        compiler_params=_cparams("arbitrary"),
        name="combine",
    )(offs, x1, pos, gate, mod, g_final.reshape(1, D_MODEL), ye)


def _trunk(x, c, w_ada, b_ada, g_mix, w_in, sink, q_gain, k_gain, nbr_bias, g_grp, w_out,
           g_ffn, w_router, w_gate, w_up, w_down, g_final):
    b = x.shape[0]
    n = b * SEQ
    cap = CAPACITY_FACTOR * n // N_EXPERTS
    depth = w_ada.shape[0]
    for i in range(depth):
        mod = _ada(c, w_ada, b_ada[i], i).reshape(b, 6, D_MODEL)
        qa, ka, va, qb, kb, vb, qc, kc, vc = _inproj(x, mod, g_mix[i], w_in, i, q_gain[i], k_gain[i])
        merged = _mixers(qa, ka, va, qb, kb, vb, qc, kc, vc, sink[i], nbr_bias[i], g_grp[i])
        x1, hn, aff = _outproj(x, merged, mod, w_out, i, g_ffn[i], w_router[i])
        pos, gate, offs = _route(aff, cap, b)
        xe = _dispatch(hn.reshape(b, SEQ, D_MODEL), pos, offs, cap)
        ye = _ffn(xe, w_gate, w_up, w_down, i, cap)
        x = _combine(x1, pos, gate, mod, g_final, ye, offs, cap, final=(i == depth - 1))
    return x


def kernel(x_prompt, x_sample, c_prompt, c_sample, w_ada, b_ada, g_mix, w_in, sink, q_gain, k_gain, rpb, g_grp,
           w_out, g_ffn, w_router, w_gate, w_up, w_down, g_final):
    w_in, w_out, w_gate, w_up, w_down = (w.astype(BF16) for w in (w_in, w_out, w_gate, w_up, w_down))
    nbr_bias = [_nbr_bias(rpb[i]) for i in range(rpb.shape[0])]
    args = (w_ada, b_ada, g_mix, w_in, sink, q_gain, k_gain, nbr_bias, g_grp, w_out,
            g_ffn, w_router, w_gate, w_up, w_down, g_final)
    return (_trunk(x_prompt, c_prompt, *args), _trunk(x_sample, c_sample, *args))
```

```python
import functools

import numpy as np
import jax
import jax.numpy as jnp
from jax import lax
from jax.experimental import pallas as pl
from jax.experimental.pallas import tpu as pltpu

F32 = jnp.float32
BF16 = jnp.bfloat16
I32 = jnp.int32

D_MODEL = 1024
SEQ = 2048
HEAD_DIM = 64
A_HEADS, A_KV = 8, 2
B_HEADS, B_KV = 4, 2
C_HEADS = 4
A_W, B_W, C_W = A_HEADS * HEAD_DIM, B_HEADS * HEAD_DIM, C_HEADS * HEAD_DIM
IN_WIDTH = 2048
WINDOW = 128
GRID_W = 64
GRID_ROWS = SEQ // GRID_W
NA_ROWS, NA_COLS = 8, 16
N_EXPERTS = 16
CAPACITY_FACTOR = 2
ROPE_THETA = 10000.0
EPS = 1e-6
NEG_INF = -1e30
Q_SCALE = HEAD_DIM ** -0.5

LANES = 128
ROW_ALIGN = 16
ROW_ALIGN_SHIFT = 4
TOK_TILE = 512
MIX_SUB = 256
MIX_TILE = 512
NBR_TILE_ROWS = 4
NBR_KEY_ROWS = NA_ROWS + NBR_TILE_ROWS - 1
ROUTE_TILE = 256
ROUTE_SEQS = 16
WIN = 64
WIN_SHIFT = 6
FFN_TILE = 1024
VMEM_LIMIT = 56 * 1024 * 1024


def _cparams(*sem):
    return pltpu.CompilerParams(dimension_semantics=sem, vmem_limit_bytes=VMEM_LIMIT)


def _split_bf16(a):
    hi = a.astype(BF16)
    lo = (a - hi.astype(F32)).astype(BF16)
    return hi, lo


def _dot(a, b):
    return jnp.dot(a, b, preferred_element_type=F32)


def _dot_nt(a, b):
    return lax.dot_general(a, b, (((1,), (1,)), ((), ())), preferred_element_type=F32)


def _dot_tn(a, b):
    return lax.dot_general(a, b, (((0,), (0,)), ((), ())), preferred_element_type=F32)


def _dot3(a, w):
    ah, al = _split_bf16(a)
    wh, wl = _split_bf16(w)
    return _dot(ah, wh) + _dot(al, wh) + _dot(ah, wl)


def _rms(x):
    return x * lax.rsqrt(jnp.mean(x * x, axis=-1, keepdims=True) + EPS)


def _ada_kernel(c_ref, w_ref, b_ref, o_ref):
    c = c_ref[...]
    a = c * jax.nn.sigmoid(c)
    o_ref[...] = _dot3(a, w_ref[0]) + b_ref[...]


def _ada(c, w_ada, b_ada, layer):
    b = c.shape[0]
    return pl.pallas_call(
        _ada_kernel,
        grid=(6,),
        in_specs=[
            pl.BlockSpec((b, D_MODEL), lambda j: (0, 0)),
            pl.BlockSpec((1, D_MODEL, D_MODEL), lambda j: (layer, 0, j)),
            pl.BlockSpec((1, D_MODEL), lambda j: (0, j)),
        ],
        out_specs=pl.BlockSpec((b, D_MODEL), lambda j: (0, j)),
        out_shape=jax.ShapeDtypeStruct((b, 6 * D_MODEL), F32),
        compiler_params=_cparams("arbitrary"),
        name="ada",
    )(c, w_ada, b_ada.reshape(1, 6 * D_MODEL))


def _rotate(x, cos, sin_signed, half):
    w = x.shape[1]
    lane = lax.broadcasted_iota(I32, x.shape, 1)
    first = (lane % (2 * half)) < half
    rot = jnp.where(first, pltpu.roll(x, w - half, 1), pltpu.roll(x, half, 1))
    return x * cos + rot * sin_signed


def _head_rms(x, hm, gain):
    ms = _dot((x * x).astype(BF16), hm) * (1.0 / HEAD_DIM)
    return x * lax.rsqrt(ms + EPS) * gain


def _store_values(v_ref, r, v):
    low = lax.broadcasted_iota(I32, (v.shape[0], LANES), 1) < HEAD_DIM
    for h in range(v.shape[1] // HEAD_DIM):
        pair = v[:, (h // 2) * LANES:(h // 2 + 1) * LANES]
        vals = pair if h % 2 == 0 else pltpu.roll(pair, HEAD_DIM, 1)
        v_ref[0, r, h * LANES:(h + 1) * LANES] = jnp.where(low, vals, 1.0).astype(BF16)


def _inproj_kernel(x_ref, mod_ref, g_ref, w_ref, ca_ref, sa_ref, cb_ref, sb_ref, qg_ref, kg_ref, hm_ref,
                   qa_ref, ka_ref, va_ref, qb_ref, kb_ref, vb_ref, qc_ref, kc_ref, vc_ref):
    parts = 2
    rows = x_ref.shape[1] // parts
    hm = hm_ref[...]

    def project(p):
        hn = _rms(x_ref[0, p * rows:(p + 1) * rows, :]) * g_ref[...]
        hn = hn * (1.0 + mod_ref[0, 1:2, :]) + mod_ref[0, 0:1, :]
        return _dot(hn.astype(BF16), w_ref[0])

    nxt = project(0)
    for p in range(parts):
        r = slice(p * rows, (p + 1) * rows)
        proj, nxt = nxt, (project(p + 1) if p + 1 < parts else None)
        ca, sa, cb, sb = ca_ref[r, :], sa_ref[r, :], cb_ref[r, :], sb_ref[r, :]
        col = 0
        for j in range(A_W // LANES):
            qa_ref[0, r, j * LANES:(j + 1) * LANES] = (
                _rotate(proj[:, col:col + LANES], ca, sa, HEAD_DIM // 2) * Q_SCALE).astype(BF16)
            col += LANES
        ka_ref[0, r, :] = _rotate(proj[:, col:col + LANES], ca, sa, HEAD_DIM // 2).astype(BF16)
        col += LANES
        _store_values(va_ref, r, proj[:, col:col + LANES])
        col += LANES
        for j in range(B_W // LANES):
            q = _head_rms(proj[:, col:col + LANES], hm, qg_ref[...])
            qb_ref[0, r, j * LANES:(j + 1) * LANES] = (_rotate(q, cb, sb, HEAD_DIM // 4) * Q_SCALE).astype(BF16)
            col += LANES
        k = _head_rms(proj[:, col:col + LANES], hm, kg_ref[...])
        kb_ref[0, r, :] = _rotate(k, cb, sb, HEAD_DIM // 4).astype(BF16)
        col += LANES
        _store_values(vb_ref, r, proj[:, col:col + LANES])
        col += LANES
        qc_ref[0, r, :] = (proj[:, col:col + C_W] * Q_SCALE).astype(BF16)
        col += C_W
        kc_ref[0, r, :] = proj[:, col:col + C_W].astype(BF16)
        col += C_W
        _store_values(vc_ref, r, proj[:, col:col + C_W])


def _rope_tables():
    t = np.arange(SEQ, dtype=np.float64)[:, None]
    j = np.arange(LANES)[None, :] % HEAD_DIM
    inv_a = ROPE_THETA ** (-(2.0 * (j % 32)) / HEAD_DIM)
    ang_a = t * inv_a
    sign_a = np.where(j < 32, -1.0, 1.0)
    inv_b = ROPE_THETA ** (-(2.0 * (j % 16)) / (HEAD_DIM // 2))
    pos_b = np.where(j < 32, np.floor(t / GRID_W), t % GRID_W)
    ang_b = pos_b * inv_b
    sign_b = np.where((j % 32) < 16, -1.0, 1.0)
    f = lambda a: jnp.asarray(a, dtype=F32)
    return f(np.cos(ang_a)), f(np.sin(ang_a) * sign_a), f(np.cos(ang_b)), f(np.sin(ang_b) * sign_b)


def _inproj(x, mod, g_mix, w_in, layer, q_gain, k_gain):
    b = x.shape[0]
    ca, sa, cb, sb = _rope_tables()
    hm = jnp.asarray(np.kron(np.eye(LANES // HEAD_DIM), np.ones((HEAD_DIM, HEAD_DIM))), dtype=BF16)
    gain2 = lambda g: jnp.tile(g, LANES // HEAD_DIM).reshape(1, LANES)
    tok = lambda w: pl.BlockSpec((1, TOK_TILE, w), lambda i, j: (i, j, 0))
    tab = pl.BlockSpec((TOK_TILE, LANES), lambda i, j: (j, 0))
    full = lambda r, c: pl.BlockSpec((r, c), lambda i, j: (0, 0))
    widths = (A_W, LANES, A_KV * LANES, B_W, LANES, B_KV * LANES, C_W, C_W, C_HEADS * LANES)
    return pl.pallas_call(
        _inproj_kernel,
        grid=(b, SEQ // TOK_TILE),
        in_specs=[
            tok(D_MODEL),
            pl.BlockSpec((1, 6, D_MODEL), lambda i, j: (i, 0, 0)),
            full(1, D_MODEL),
            pl.BlockSpec((1, D_MODEL, IN_WIDTH), lambda i, j: (layer, 0, 0)),
            tab, tab, tab, tab,
            full(1, LANES), full(1, LANES), full(LANES, LANES),
        ],
        out_specs=[tok(w) for w in widths],
        out_shape=[jax.ShapeDtypeStruct((b, SEQ, w), BF16) for w in widths],
        compiler_params=_cparams("arbitrary", "arbitrary"),
        name="inproj",
    )(x, mod, g_mix.reshape(1, D_MODEL), w_in, ca, sa, cb, sb, gain2(q_gain), gain2(k_gain), hm)


def _softmax_pv_aug(s, v_aug, extra=None):
    m = jnp.max(s, axis=-1, keepdims=True)
    if extra is not None:
        m = jnp.maximum(m, extra)
    p = jnp.exp((s - m).astype(BF16))
    o = _dot(p, v_aug)
    l = o[:, HEAD_DIM:HEAD_DIM + 1]
    if extra is not None:
        l = l + jnp.exp(extra - m)
    return o[:, 0:HEAD_DIM] * (1.0 / l)


def _window_masks():
    rows = (A_HEADS // A_KV) * WINDOW
    qi = np.arange(rows)[:, None] % WINDOW
    ki = np.arange(3 * WINDOW)[None, :]
    masks = [np.where(np.abs(ki - qi - shift * WINDOW) <= WINDOW, 0.0, NEG_INF) for shift in range(3)]
    return jnp.asarray(np.stack(masks), dtype=F32)


def _nbr_bias(rpb):
    edge = GRID_W - NA_COLS
    padded = jnp.pad(rpb, ((0, 0), (0, 0), (edge, edge)))
    toe = jnp.stack([padded[..., GRID_W - 1 - c:2 * GRID_W - 1 - c] for c in range(GRID_W)], axis=2)
    c = np.arange(GRID_W)[:, None]
    kc = np.arange(GRID_W)[None, :]
    cs = np.clip(c - NA_COLS // 2, 0, GRID_W - NA_COLS)
    toe = jnp.where((kc >= cs) & (kc < cs + NA_COLS), toe, NEG_INF)
    masked = jnp.full((C_HEADS, GRID_W, GRID_W), NEG_INF, F32)
    ntiles = GRID_ROWS // NBR_TILE_ROWS
    classes = []
    for r0 in (0, NBR_TILE_ROWS, (ntiles - 1) * NBR_TILE_ROWS):
        ws = min(max(r0 - NA_ROWS // 2, 0), GRID_ROWS - NBR_KEY_ROWS)
        per_query_row = []
        for rq in range(NBR_TILE_ROWS):
            r = r0 + rq
            rs = min(max(r - NA_ROWS // 2, 0), GRID_ROWS - NA_ROWS)
            blocks = [toe[:, ws + kr - r + NA_ROWS - 1] if rs <= ws + kr < rs + NA_ROWS else masked
                      for kr in range(NBR_KEY_ROWS)]
            per_query_row.append(jnp.stack(blocks, axis=2))
        classes.append(jnp.stack(per_query_row, axis=1))
    return jnp.stack(classes).reshape(3, C_HEADS, NBR_TILE_ROWS * GRID_W, NBR_KEY_ROWS * GRID_W)


def _mixers_kernel(sink_ref, qa_ref, ka_ref, va_ref, qb_ref, kb_ref, vb_ref, qc_ref, kc_ref, vc_ref,
                   g_ref, mask_ref, bias_ref, o_ref):
    j = pl.program_id(1)
    span = WINDOW * 3
    group_a = A_HEADS // A_KV
    rows_a = group_a * WINDOW
    nblocks = SEQ // WINDOW
    head_of_row = lax.broadcasted_iota(I32, (rows_a, 1), 0) // WINDOW
    sinks = []
    for kv in range(A_KV):
        sink = jnp.zeros((rows_a, 1), F32)
        for h in range(group_a):
            sink = jnp.where(head_of_row == h, sink_ref[kv * group_a + h], sink)
        sinks.append(sink)
    outs_a, outs_b, outs_c = {}, {}, {}

    def window_task(blk, kv):
        i = j * (MIX_TILE // WINDOW) + blk
        kstart = pl.multiple_of(jnp.clip(i * WINDOW - WINDOW, 0, SEQ - span), WINDOW)
        case = jnp.where(i == 0, 0, jnp.where(i == nblocks - 1, 2, 1))

        def score():
            qs = jnp.concatenate([qa_ref[0, blk * WINDOW:(blk + 1) * WINDOW, h * HEAD_DIM:(h + 1) * HEAD_DIM]
                                  for h in range(kv * group_a, (kv + 1) * group_a)], axis=0)
            return _dot_nt(qs, ka_ref[0, pl.ds(kstart, span), kv * HEAD_DIM:(kv + 1) * HEAD_DIM]) + mask_ref[case]

        def finish(s):
            o = _softmax_pv_aug(s, va_ref[0, pl.ds(kstart, span), kv * LANES:(kv + 1) * LANES], extra=sinks[kv])
            outs_a[blk, kv] = [o[h * WINDOW:(h + 1) * WINDOW] for h in range(group_a)]
        return score, finish

    nsub = MIX_TILE // MIX_SUB
    sub_rows = lambda sub: slice(sub * MIX_SUB, (sub + 1) * MIX_SUB)

    def dense_task(sub, h):
        kv = h // (B_HEADS // B_KV)

        def score():
            return _dot_nt(qb_ref[0, sub_rows(sub), h * HEAD_DIM:(h + 1) * HEAD_DIM],
                           kb_ref[0, :, kv * HEAD_DIM:(kv + 1) * HEAD_DIM])

        def finish(s):
            outs_b[sub, h] = _softmax_pv_aug(s, vb_ref[0, :, kv * LANES:(kv + 1) * LANES])
        return score, finish

    ntiles = GRID_ROWS // NBR_TILE_ROWS
    nkeys = NBR_KEY_ROWS * GRID_W

    def nbr_task(sub, h):
        tile = j * nsub + sub
        ws = jnp.clip(tile * NBR_TILE_ROWS - NA_ROWS // 2, 0, GRID_ROWS - NBR_KEY_ROWS)
        cls = jnp.where(tile == 0, 0, jnp.where(tile == ntiles - 1, 2, 1))
        kstart = pl.multiple_of(ws * GRID_W, GRID_W)
        cols = slice(h * HEAD_DIM, (h + 1) * HEAD_DIM)

        def score():
            return _dot_nt(qc_ref[0, sub_rows(sub), cols], kc_ref[0, pl.ds(kstart, nkeys), cols]) + bias_ref[cls, h]

        def finish(s):
            outs_c[sub, h] = _softmax_pv_aug(s, vc_ref[0, pl.ds(kstart, nkeys), h * LANES:(h + 1) * LANES])
        return score, finish

    window = [window_task(blk, kv) for blk in range(MIX_TILE // WINDOW) for kv in range(A_KV)]
    per_sub = len(window) // nsub
    tasks = []
    for sub in range(nsub):
        for t in range(per_sub):
            tasks += [dense_task(sub, t), window[sub * per_sub + t], nbr_task(sub, t)]
    scores = tasks[0][0]()
    for t, (_, finish) in enumerate(tasks):
        cur, scores = scores, (tasks[t + 1][0]() if t + 1 < len(tasks) else None)
        finish(cur)

    g = g_ref[...]
    for blk in range(MIX_TILE // WINDOW):
        o = jnp.concatenate(outs_a[blk, 0] + outs_a[blk, 1], axis=1)
        o_ref[0, blk * WINDOW:(blk + 1) * WINDOW, 0:A_W] = (_rms(o) * g[:, 0:A_W]).astype(BF16)
    for sub in range(nsub):
        o = jnp.concatenate([outs_b[sub, h] for h in range(B_HEADS)], axis=1)
        o_ref[0, sub_rows(sub), A_W:A_W + B_W] = (_rms(o) * g[:, A_W:A_W + B_W]).astype(BF16)
        o = jnp.concatenate([outs_c[sub, h] for h in range(C_HEADS)], axis=1)
        o_ref[0, sub_rows(sub), A_W + B_W:] = (_rms(o) * g[:, A_W + B_W:]).astype(BF16)


def _mixers(qa, ka, va, qb, kb, vb, qc, kc, vc, sink, bias, g_grp):
    b = qa.shape[0]
    til = lambda w: pl.BlockSpec((1, MIX_TILE, w), lambda i, j: (i, j, 0))
    seq = lambda w: pl.BlockSpec((1, SEQ, w), lambda i, j: (i, 0, 0))
    const = lambda a: pl.BlockSpec(a.shape, lambda i, j: (0,) * a.ndim)
    masks = _window_masks()
    width = A_W + B_W + C_W
    return pl.pallas_call(
        _mixers_kernel,
        grid=(b, SEQ // MIX_TILE),
        in_specs=[
            pl.BlockSpec(memory_space=pltpu.SMEM),
            til(A_W), seq(LANES), seq(A_KV * LANES),
            til(B_W), seq(LANES), seq(B_KV * LANES),
            til(C_W), seq(C_W), seq(C_HEADS * LANES),
            pl.BlockSpec((1, width), lambda i, j: (0, 0)),
            const(masks), const(bias),
        ],
        out_specs=til(width),
        out_shape=jax.ShapeDtypeStruct((b, SEQ, width), BF16),
        compiler_params=_cparams("arbitrary", "arbitrary"),
        name="mixers",
    )(sink, qa, ka, va, qb, kb, vb, qc, kc, vc, g_grp.reshape(1, width), masks, bias)


def _outproj_kernel(x_ref, m_ref, mod_ref, wo_ref, g_ref, wr_ref, x1_ref, hn_ref, aff_ref):
    parts = 2
    rows = x_ref.shape[1] // parts
    mixed = lambda p: _dot(m_ref[0, p * rows:(p + 1) * rows, :], wo_ref[0])

    nxt = mixed(0)
    for p in range(parts):
        r = slice(p * rows, (p + 1) * rows)
        attn, nxt = nxt, (mixed(p + 1) if p + 1 < parts else None)
        x1 = x_ref[0, r, :] + mod_ref[0, 2:3, :] * attn
        x1_ref[0, r, :] = x1
        hn = _rms(x1) * g_ref[...]
        hn = hn * (1.0 + mod_ref[0, 4:5, :]) + mod_ref[0, 3:4, :]
        hn_ref[r, :] = hn.astype(BF16)
        hh, hl = _split_bf16(hn)
        both = _dot(hh, wr_ref[...])
        logits = both + pltpu.roll(both, LANES - N_EXPERTS, 1) + _dot(hl, wr_ref[...])
        lt = logits.T[0:N_EXPERTS, :]
        e = jnp.exp(lt - jnp.max(lt, axis=0, keepdims=True))
        aff_ref[:, r] = e * (1.0 / jnp.sum(e, axis=0, keepdims=True))


def _outproj(x, merged, mod, w_out, layer, g_ffn, w_router):
    b = x.shape[0]
    n = b * SEQ
    nt = SEQ // TOK_TILE
    tok = lambda w: pl.BlockSpec((1, TOK_TILE, w), lambda i, j: (i, j, 0))
    wr_hi, wr_lo = _split_bf16(w_router)
    wr = jnp.pad(jnp.concatenate([wr_hi, wr_lo], axis=1), ((0, 0), (0, LANES - 2 * N_EXPERTS)))
    return pl.pallas_call(
        _outproj_kernel,
        grid=(b, nt),
        in_specs=[
            tok(D_MODEL), tok(A_W + B_W + C_W),
            pl.BlockSpec((1, 6, D_MODEL), lambda i, j: (i, 0, 0)),
            pl.BlockSpec((1, D_MODEL, D_MODEL), lambda i, j: (layer, 0, 0)),
            pl.BlockSpec((1, D_MODEL), lambda i, j: (0, 0)),
            pl.BlockSpec((D_MODEL, LANES), lambda i, j: (0, 0)),
        ],
        out_specs=[
            tok(D_MODEL),
            pl.BlockSpec((TOK_TILE, D_MODEL), lambda i, j: (i * nt + j, 0)),
            pl.BlockSpec((N_EXPERTS, TOK_TILE), lambda i, j: (0, i * nt + j)),
        ],
        out_shape=[
            jax.ShapeDtypeStruct((b, SEQ, D_MODEL), F32),
            jax.ShapeDtypeStruct((n, D_MODEL), BF16),
            jax.ShapeDtypeStruct((N_EXPERTS, n), F32),
        ],
        compiler_params=_cparams("arbitrary", "arbitrary"),
        name="outproj_router",
    )(x, merged, mod, w_out, g_ffn.reshape(1, D_MODEL), wr)


def _route_kernel(aff_ref, pos_ref, rp_ref, *, cap):
    nr = aff_ref.shape[1]
    as_bits = lambda a: lax.bitcast_convert_type(a, I32)

    def bisect(_, carry):
        lo, hi = carry
        mid = lo + ((hi - lo) >> 1)
        ge = jnp.where(as_bits(aff_ref[...]) >= mid, 1.0, 0.0)
        cnt = jnp.sum(jnp.sum(ge, axis=1, keepdims=True), axis=2, keepdims=True)
        keep = cnt >= cap
        return jnp.where(keep, mid, lo), jnp.where(keep, hi, mid)

    lo0 = jnp.zeros((N_EXPERTS, 1, 1), I32)
    hi0 = jnp.full((N_EXPERTS, 1, 1), 0x7F800000, I32)
    thr, _ = lax.fori_loop(0, 31, bisect, (lo0, hi0))

    tri = jnp.where(lax.broadcasted_iota(I32, (LANES, LANES), 0) <= lax.broadcasted_iota(I32, (LANES, LANES), 1),
                    1.0, 0.0).astype(BF16)
    below = jnp.where(lax.broadcasted_iota(I32, (nr, nr), 1) < lax.broadcasted_iota(I32, (nr, nr), 0),
                      1.0, 0.0).astype(BF16)

    def ranks(mask):
        incl = _dot(mask.astype(BF16), tri)
        before = _dot(below, incl.astype(BF16))[:, LANES - 1:]
        return incl - mask + before, before

    for e in range(N_EXPERTS):
        b = as_bits(aff_ref[e])
        t = thr[e]
        gt = b > t
        eq = b == t
        n_gt = jnp.sum(jnp.sum(jnp.where(gt, 1.0, 0.0), axis=0, keepdims=True), axis=1, keepdims=True)
        eq_rank, _ = ranks(jnp.where(eq, 1.0, 0.0))
        sel = jnp.where(gt, 1.0, jnp.where(eq, jnp.where(eq_rank < cap - n_gt, 1.0, 0.0), 0.0))
        pos, before = ranks(sel)
        pos_ref[e] = jnp.where(sel > 0.5, pos, -1.0).astype(I32)
        rp_ref[e] = before


def _block_geometry(b):
    sb = min(b, ROUTE_SEQS)
    return sb, ROUTE_TILE // sb, b // sb


def _route(aff_t, cap, b):
    n = aff_t.shape[1]
    nr = n // LANES
    sb, tt, groups = _block_geometry(b)
    aff_t = aff_t.reshape(N_EXPERTS, groups, sb, SEQ // tt, tt).transpose(0, 3, 1, 2, 4).reshape(N_EXPERTS, n)
    pos, rp = pl.pallas_call(
        functools.partial(_route_kernel, cap=cap),
        out_shape=[
            jax.ShapeDtypeStruct((N_EXPERTS, nr, LANES), I32),
            jax.ShapeDtypeStruct((N_EXPERTS, nr, 1), F32),
        ],
        compiler_params=pltpu.CompilerParams(vmem_limit_bytes=VMEM_LIMIT),
        name="route",
    )(aff_t.reshape(N_EXPERTS, nr, LANES))
    per_block = ROUTE_TILE // LANES
    offs = rp[:, ::per_block, 0].astype(I32)
    offs = jnp.concatenate([offs, jnp.full((N_EXPERTS, 1), cap, I32)], axis=1)
    return pos.reshape(N_EXPERTS, n), aff_t, offs.reshape(-1)


def _align(v):
    return pl.multiple_of(lax.shift_left(lax.shift_right_logical(v, ROW_ALIGN_SHIFT), ROW_ALIGN_SHIFT), ROW_ALIGN)


def _dispatch_kernel(offs_ref, hn_ref, pos_ref, xe_ref, stage, ostage, carry, sem, osem, *, nb, cap):
    t = pl.program_id(0)
    slot = t % 2
    off = lambda e, tt: offs_ref[e * (nb + 1) + tt]
    rows = lax.broadcasted_iota(I32, (WIN, ROUTE_TILE), 0)
    pos = pos_ref[...]
    hn = hn_ref[...].reshape(ROUTE_TILE, D_MODEL)

    def main_copy(e, tt, sl):
        return pltpu.make_async_copy(stage.at[sl, e * WIN:(e + 1) * WIN],
                                     xe_ref.at[e, pl.ds(_align(off(e, tt)), WIN)], sem.at[e])

    @pl.when(t == 0)
    def _():
        carry[...] = jnp.zeros_like(carry)

    onehot = jnp.concatenate(
        [jnp.where(pos[e:e + 1, :] - _align(off(e, t)) == rows, 1.0, 0.0) for e in range(N_EXPERTS)],
        axis=0).astype(BF16)
    stage[slot] = _dot(onehot, hn).astype(BF16)

    starts = [_align(off(e, t)) for e in range(N_EXPERTS)]
    filled = [off(e, t + 1) - starts[e] for e in range(N_EXPERTS)]
    nwin = [lax.shift_right_logical(f, WIN_SHIFT) + 1 for f in filled]
    tail = [_align(filled[e]) - (nwin[e] - 1) * WIN for e in range(N_EXPERTS)]

    for e in range(N_EXPERTS):
        first = stage[slot, e * WIN:e * WIN + ROW_ALIGN, :]
        stage[slot, e * WIN:e * WIN + ROW_ALIGN, :] = first + carry[e]

    @pl.when(t > 0)
    def _():
        for e in range(N_EXPERTS):
            main_copy(e, t - 1, 1 - slot).wait()

    for e in range(N_EXPERTS):
        main_copy(e, t, slot).start()
        row = e * WIN + jnp.where(nwin[e] == 1, tail[e], 0)
        carry[e] = stage[slot, pl.ds(pl.multiple_of(row, ROW_ALIGN), ROW_ALIGN), :]

    extra_windows = sum(nwin) - N_EXPERTS

    @pl.when(extra_windows > 0)
    def _():
        for e in range(N_EXPERTS):
            def extra(c, unused):
                lo = pl.multiple_of(starts[e] + c * WIN, ROW_ALIGN)
                win_rows = lax.broadcasted_iota(I32, (WIN, ROUTE_TILE), 0)
                oh = jnp.where(pos_ref[e:e + 1, :] - lo == win_rows, 1.0, 0.0).astype(BF16)
                ostage[...] = _dot(oh, hn_ref[...].reshape(ROUTE_TILE, D_MODEL)).astype(BF16)
                cp = pltpu.make_async_copy(ostage, xe_ref.at[e, pl.ds(lo, WIN)], osem)
                cp.start()
                cp.wait()

                @pl.when(c == nwin[e] - 1)
                def _():
                    carry[e] = ostage[pl.ds(pl.multiple_of(tail[e], ROW_ALIGN), ROW_ALIGN), :]
                return unused

            lax.fori_loop(1, nwin[e], extra, 0)

    @pl.when(t == nb - 1)
    def _():
        for e in range(N_EXPERTS):
            main_copy(e, t, slot).wait()
        stage[1 - slot, 0:WIN, :] = jnp.zeros((WIN, D_MODEL), BF16)
        pads = [pltpu.make_async_copy(stage.at[1 - slot, 0:WIN], xe_ref.at[e, cap:cap + WIN], sem.at[e])
                for e in range(N_EXPERTS)]
        for cp in pads:
            cp.start()
        for cp in pads:
            cp.wait()


def _dispatch(hn, pos, offs, cap):
    b = hn.shape[0]
    nb = b * SEQ // ROUTE_TILE
    sb, tt, groups = _block_geometry(b)
    return pl.pallas_call(
        functools.partial(_dispatch_kernel, nb=nb, cap=cap),
        grid_spec=pltpu.PrefetchScalarGridSpec(
            num_scalar_prefetch=1,
            grid=(nb,),
            in_specs=[
                pl.BlockSpec((sb, tt, D_MODEL), lambda t, o: (t % groups, t // groups, 0)),
                pl.BlockSpec((N_EXPERTS, ROUTE_TILE), lambda t, o: (0, t)),
            ],
            out_specs=pl.BlockSpec(memory_space=pl.ANY),
            scratch_shapes=[
                pltpu.VMEM((2, N_EXPERTS * WIN, D_MODEL), BF16),
                pltpu.VMEM((WIN, D_MODEL), BF16),
                pltpu.VMEM((N_EXPERTS, ROW_ALIGN, D_MODEL), BF16),
                pltpu.SemaphoreType.DMA((N_EXPERTS,)),
                pltpu.SemaphoreType.DMA(()),
            ],
        ),
        out_shape=jax.ShapeDtypeStruct((N_EXPERTS, cap + WIN, D_MODEL), BF16),
        compiler_params=_cparams("arbitrary"),
        name="dispatch",
    )(offs, hn, pos)


def _ffn_kernel(xe_ref, wg_ref, wu_ref, wd_ref, ye_ref):
    x = xe_ref[0]
    g = _dot(x, wg_ref[0, 0])
    u = _dot(x, wu_ref[0, 0])
    hid = (g * jax.nn.sigmoid(g) * u).astype(BF16)
    ye_ref[0] = _dot(hid, wd_ref[0, 0]).astype(BF16)


def _ffn(xe, w_gate, w_up, w_down, layer, cap):
    tile = min(FFN_TILE, cap)
    til = pl.BlockSpec((1, tile, D_MODEL), lambda e, j: (e, j, 0))
    wsp = pl.BlockSpec((1, 1, D_MODEL, D_MODEL), lambda e, j: (layer, e, 0, 0))
    return pl.pallas_call(
        _ffn_kernel,
        grid=(N_EXPERTS, cap // tile),
        in_specs=[til, wsp, wsp, wsp],
        out_specs=til,
        out_shape=jax.ShapeDtypeStruct((N_EXPERTS, cap, D_MODEL), BF16),
        compiler_params=_cparams("arbitrary", "arbitrary"),
        name="expert_ffn",
    )(xe, w_gate, w_up, w_down)


def _combine_kernel(offs_ref, x_ref, pos_ref, gate_ref, mod_ref, gfin_ref, ye_ref, o_ref,
                    yin, obuf, yacc, sem, osem, *, nb, cap, final):
    t = pl.program_id(0)
    slot = t % 2
    off = lambda e, tt: offs_ref[e * (nb + 1) + tt]
    base = lambda lo: pl.multiple_of(jnp.minimum(lo, cap - WIN), ROW_ALIGN)
    slots = lax.broadcasted_iota(I32, (WIN, ROUTE_TILE), 0)

    def main_copy(e, tt, sl):
        return pltpu.make_async_copy(ye_ref.at[e, pl.ds(base(_align(off(e, tt))), WIN)],
                                     yin.at[sl, e * WIN:(e + 1) * WIN], sem.at[sl, e])

    @pl.when(t == 0)
    def _():
        for e in range(N_EXPERTS):
            main_copy(e, 0, 0).start()

    ahead = jnp.minimum(t + 1, nb - 1)
    for e in range(N_EXPERTS):
        main_copy(e, ahead, 1 - slot).start()

    pos = pos_ref[...]
    gate = gate_ref[...]

    def weights(e, lo):
        p = pos[e:e + 1, :]
        p = jnp.where(p >= lo, p, -1)
        return jnp.where(p - base(lo) == slots, gate[e:e + 1, :], 0.0).astype(BF16)

    gmat = jnp.concatenate([weights(e, _align(off(e, t))) for e in range(N_EXPERTS)], axis=0)
    for e in range(N_EXPERTS):
        main_copy(e, t, slot).wait()
    yacc[...] = _dot_tn(gmat, yin[slot])

    starts = [_align(off(e, t)) for e in range(N_EXPERTS)]
    nwin = [lax.shift_right_logical(jnp.maximum(off(e, t + 1) - starts[e] - 1, 0), WIN_SHIFT) + 1
            for e in range(N_EXPERTS)]
    extra_windows = sum(nwin) - N_EXPERTS

    @pl.when(extra_windows > 0)
    def _():
        for e in range(N_EXPERTS):
            def extra(c, carry):
                lo = starts[e] + c * WIN
                cp = pltpu.make_async_copy(ye_ref.at[e, pl.ds(base(lo), WIN)], obuf, osem)
                cp.start()
                cp.wait()
                yacc[...] += _dot_tn(weights(e, lo), obuf[...])
                return carry

            lax.fori_loop(1, nwin[e], extra, 0)

    y = x_ref[...] + mod_ref[:, 5:6, :] * yacc[...].reshape(x_ref.shape)
    if final:
        y = _rms(y) * gfin_ref[...]
    o_ref[...] = y

    @pl.when(t == nb - 1)
    def _():
        for e in range(N_EXPERTS):
            main_copy(e, ahead, 1 - slot).wait()


def _combine(x1, pos, gate, mod, g_final, ye, offs, cap, final):
    b = x1.shape[0]
    nb = b * SEQ // ROUTE_TILE
    sb, tt, groups = _block_geometry(b)
    tok = pl.BlockSpec((sb, tt, D_MODEL), lambda t, o: (t % groups, t // groups, 0))
    rt = pl.BlockSpec((N_EXPERTS, ROUTE_TILE), lambda t, o: (0, t))
    return pl.pallas_call(
        functools.partial(_combine_kernel, nb=nb, cap=cap, final=final),
        grid_spec=pltpu.PrefetchScalarGridSpec(
            num_scalar_prefetch=1,
            grid=(nb,),
            in_specs=[
                tok, rt, rt,
                pl.BlockSpec((sb, 6, D_MODEL), lambda t, o: (t % groups, 0, 0)),
                pl.BlockSpec((1, D_MODEL), lambda t, o: (0, 0)),
                pl.BlockSpec(memory_space=pl.ANY),
            ],
            out_specs=tok,
            scratch_shapes=[
                pltpu.VMEM((2, N_EXPERTS * WIN, D_MODEL), BF16),
                pltpu.VMEM((WIN, D_MODEL), BF16),
                pltpu.VMEM((ROUTE_TILE, D_MODEL), F32),
                pltpu.SemaphoreType.DMA((2, N_EXPERTS)),
                pltpu.SemaphoreType.DMA(()),
            ],
        ),
        out_shape=jax.ShapeDtypeStruct((b, SEQ, D_MODEL), F32),
        compiler_params=_cparams("arbitrary"),
        name="combine",
    )(offs, x1, pos, gate, mod, g_final.reshape(1, D_MODEL), ye)


def _trunk(x, c, w_ada, b_ada, g_mix, w_in, sink, q_gain, k_gain, nbr_bias, g_grp, w_out,
           g_ffn, w_router, w_gate, w_up, w_down, g_final):
    b = x.shape[0]
    n = b * SEQ
    cap = CAPACITY_FACTOR * n // N_EXPERTS
    depth = w_ada.shape[0]
    for i in range(depth):
        mod = _ada(c, w_ada, b_ada[i], i).reshape(b, 6, D_MODEL)
        qa, ka, va, qb, kb, vb, qc, kc, vc = _inproj(x, mod, g_mix[i], w_in, i, q_gain[i], k_gain[i])
        merged = _mixers(qa, ka, va, qb, kb, vb, qc, kc, vc, sink[i], nbr_bias[i], g_grp[i])
        x1, hn, aff = _outproj(x, merged, mod, w_out, i, g_ffn[i], w_router[i])
        pos, gate, offs = _route(aff, cap, b)
        xe = _dispatch(hn.reshape(b, SEQ, D_MODEL), pos, offs, cap)
        ye = _ffn(xe, w_gate, w_up, w_down, i, cap)
        x = _combine(x1, pos, gate, mod, g_final, ye, offs, cap, final=(i == depth - 1))
    return x


def kernel(x_prompt, x_sample, c_prompt, c_sample, w_ada, b_ada, g_mix, w_in, sink, q_gain, k_gain, rpb, g_grp,
           w_out, g_ffn, w_router, w_gate, w_up, w_down, g_final):
    w_in, w_out, w_gate, w_up, w_down = (w.astype(BF16) for w in (w_in, w_out, w_gate, w_up, w_down))
    nbr_bias = [_nbr_bias(rpb[i]) for i in range(rpb.shape[0])]
    args = (w_ada, b_ada, g_mix, w_in, sink, q_gain, k_gain, nbr_bias, g_grp, w_out,
            g_ffn, w_router, w_gate, w_up, w_down, g_final)
    return (_trunk(x_prompt, c_prompt, *args), _trunk(x_sample, c_sample, *args))
```

```python
import functools

import numpy as np
import jax
import jax.numpy as jnp
from jax import lax
from jax.experimental import pallas as pl
from jax.experimental.pallas import tpu as pltpu

F32 = jnp.float32
BF16 = jnp.bfloat16
I32 = jnp.int32

D_MODEL = 1024
SEQ = 2048
HEAD_DIM = 64
A_HEADS, A_KV = 8, 2
B_HEADS, B_KV = 4, 2
C_HEADS = 4
A_W, B_W, C_W = A_HEADS * HEAD_DIM, B_HEADS * HEAD_DIM, C_HEADS * HEAD_DIM
IN_WIDTH = 2048
WINDOW = 128
GRID_W = 64
GRID_ROWS = SEQ // GRID_W
NA_ROWS, NA_COLS = 8, 16
N_EXPERTS = 16
CAPACITY_FACTOR = 2
ROPE_THETA = 10000.0
EPS = 1e-6
NEG_INF = -1e30
Q_SCALE = HEAD_DIM ** -0.5

LANES = 128
ROW_ALIGN = 16
ROW_ALIGN_SHIFT = 4
TOK_TILE = 512
MIX_SUB = 256
MIX_TILE = 512
NBR_TILE_ROWS = 4
NBR_KEY_ROWS = NA_ROWS + NBR_TILE_ROWS - 1
ROUTE_TILE = 256
ROUTE_SEQS = 16
BISECT_STEPS = 64
WIN = 64
WIN_SHIFT = 6
FFN_TILE = 1024
VMEM_LIMIT = 56 * 1024 * 1024


def _cparams(*sem):
    return pltpu.CompilerParams(dimension_semantics=sem, vmem_limit_bytes=VMEM_LIMIT)


def _split_bf16(a):
    hi = a.astype(BF16)
    lo = (a - hi.astype(F32)).astype(BF16)
    return hi, lo


def _dot(a, b):
    return jnp.dot(a, b, preferred_element_type=F32)


def _dot_nt(a, b):
    return lax.dot_general(a, b, (((1,), (1,)), ((), ())), preferred_element_type=F32)


def _dot_tn(a, b):
    return lax.dot_general(a, b, (((0,), (0,)), ((), ())), preferred_element_type=F32)


def _dot3(a, w):
    ah, al = _split_bf16(a)
    wh, wl = _split_bf16(w)
    return _dot(ah, wh) + _dot(al, wh) + _dot(ah, wl)


def _rms(x):
    return x * lax.rsqrt(jnp.mean(x * x, axis=-1, keepdims=True) + EPS)


def _ada_kernel(c_ref, w_ref, b_ref, o_ref):
    c = c_ref[...]
    a = c * jax.nn.sigmoid(c)
    o_ref[...] = _dot3(a, w_ref[0]) + b_ref[...]


def _ada(c, w_ada, b_ada, layer):
    b = c.shape[0]
    return pl.pallas_call(
        _ada_kernel,
        grid=(6,),
        in_specs=[
            pl.BlockSpec((b, D_MODEL), lambda j: (0, 0)),
            pl.BlockSpec((1, D_MODEL, D_MODEL), lambda j: (layer, 0, j)),
            pl.BlockSpec((1, D_MODEL), lambda j: (0, j)),
        ],
        out_specs=pl.BlockSpec((b, D_MODEL), lambda j: (0, j)),
        out_shape=jax.ShapeDtypeStruct((b, 6 * D_MODEL), F32),
        compiler_params=_cparams("arbitrary"),
        name="ada",
    )(c, w_ada, b_ada.reshape(1, 6 * D_MODEL))


def _rotate(x, cos, sin_signed, half):
    w = x.shape[1]
    lane = lax.broadcasted_iota(I32, x.shape, 1)
    first = (lane % (2 * half)) < half
    rot = jnp.where(first, pltpu.roll(x, w - half, 1), pltpu.roll(x, half, 1))
    return x * cos + rot * sin_signed


def _head_rms(x, hm, gain):
    ms = _dot((x * x).astype(BF16), hm) * (1.0 / HEAD_DIM)
    return x * lax.rsqrt(ms + EPS) * gain


def _store_values(v_ref, r, v):
    low = lax.broadcasted_iota(I32, (v.shape[0], LANES), 1) < HEAD_DIM
    for h in range(v.shape[1] // HEAD_DIM):
        pair = v[:, (h // 2) * LANES:(h // 2 + 1) * LANES]
        vals = pair if h % 2 == 0 else pltpu.roll(pair, HEAD_DIM, 1)
        v_ref[0, r, h * LANES:(h + 1) * LANES] = jnp.where(low, vals, 1.0).astype(BF16)


def _inproj_kernel(x_ref, mod_ref, g_ref, w_ref, ca_ref, sa_ref, cb_ref, sb_ref, qg_ref, kg_ref, hm_ref,
                   qa_ref, ka_ref, va_ref, qb_ref, kb_ref, vb_ref, qc_ref, kc_ref, vc_ref):
    parts = 2
    rows = x_ref.shape[1] // parts
    hm = hm_ref[...]

    def project(p):
        hn = _rms(x_ref[0, p * rows:(p + 1) * rows, :]) * g_ref[...]
        hn = hn * (1.0 + mod_ref[0, 1:2, :]) + mod_ref[0, 0:1, :]
        return _dot(hn.astype(BF16), w_ref[0])

    nxt = project(0)
    for p in range(parts):
        r = slice(p * rows, (p + 1) * rows)
        proj, nxt = nxt, (project(p + 1) if p + 1 < parts else None)
        ca, sa, cb, sb = ca_ref[r, :], sa_ref[r, :], cb_ref[r, :], sb_ref[r, :]
        col = 0
        for j in range(A_W // LANES):
            qa_ref[0, r, j * LANES:(j + 1) * LANES] = (
                _rotate(proj[:, col:col + LANES], ca, sa, HEAD_DIM // 2) * Q_SCALE).astype(BF16)
            col += LANES
        ka_ref[0, r, :] = _rotate(proj[:, col:col + LANES], ca, sa, HEAD_DIM // 2).astype(BF16)
        col += LANES
        _store_values(va_ref, r, proj[:, col:col + LANES])
        col += LANES
        for j in range(B_W // LANES):
            q = _head_rms(proj[:, col:col + LANES], hm, qg_ref[...])
            qb_ref[0, r, j * LANES:(j + 1) * LANES] = (_rotate(q, cb, sb, HEAD_DIM // 4) * Q_SCALE).astype(BF16)
            col += LANES
        k = _head_rms(proj[:, col:col + LANES], hm, kg_ref[...])
        kb_ref[0, r, :] = _rotate(k, cb, sb, HEAD_DIM // 4).astype(BF16)
        col += LANES
        _store_values(vb_ref, r, proj[:, col:col + LANES])
        col += LANES
        qc_ref[0, r, :] = (proj[:, col:col + C_W] * Q_SCALE).astype(BF16)
        col += C_W
        kc_ref[0, r, :] = proj[:, col:col + C_W].astype(BF16)
        col += C_W
        _store_values(vc_ref, r, proj[:, col:col + C_W])


def _rope_tables():
    t = np.arange(SEQ, dtype=np.float64)[:, None]
    j = np.arange(LANES)[None, :] % HEAD_DIM
    inv_a = ROPE_THETA ** (-(2.0 * (j % 32)) / HEAD_DIM)
    ang_a = t * inv_a
    sign_a = np.where(j < 32, -1.0, 1.0)
    inv_b = ROPE_THETA ** (-(2.0 * (j % 16)) / (HEAD_DIM // 2))
    pos_b = np.where(j < 32, np.floor(t / GRID_W), t % GRID_W)
    ang_b = pos_b * inv_b
    sign_b = np.where((j % 32) < 16, -1.0, 1.0)
    f = lambda a: jnp.asarray(a, dtype=F32)
    return f(np.cos(ang_a)), f(np.sin(ang_a) * sign_a), f(np.cos(ang_b)), f(np.sin(ang_b) * sign_b)


def _inproj(x, mod, g_mix, w_in, layer, q_gain, k_gain):
    b = x.shape[0]
    ca, sa, cb, sb = _rope_tables()
    hm = jnp.asarray(np.kron(np.eye(LANES // HEAD_DIM), np.ones((HEAD_DIM, HEAD_DIM))), dtype=BF16)
    gain2 = lambda g: jnp.tile(g, LANES // HEAD_DIM).reshape(1, LANES)
    tok = lambda w: pl.BlockSpec((1, TOK_TILE, w), lambda i, j: (i, j, 0))
    tab = pl.BlockSpec((TOK_TILE, LANES), lambda i, j: (j, 0))
    full = lambda r, c: pl.BlockSpec((r, c), lambda i, j: (0, 0))
    widths = (A_W, LANES, A_KV * LANES, B_W, LANES, B_KV * LANES, C_W, C_W, C_HEADS * LANES)
    return pl.pallas_call(
        _inproj_kernel,
        grid=(b, SEQ // TOK_TILE),
        in_specs=[
            tok(D_MODEL),
            pl.BlockSpec((1, 6, D_MODEL), lambda i, j: (i, 0, 0)),
            full(1, D_MODEL),
            pl.BlockSpec((1, D_MODEL, IN_WIDTH), lambda i, j: (layer, 0, 0)),
            tab, tab, tab, tab,
            full(1, LANES), full(1, LANES), full(LANES, LANES),
        ],
        out_specs=[tok(w) for w in widths],
        out_shape=[jax.ShapeDtypeStruct((b, SEQ, w), BF16) for w in widths],
        compiler_params=_cparams("arbitrary", "arbitrary"),
        name="inproj",
    )(x, mod, g_mix.reshape(1, D_MODEL), w_in, ca, sa, cb, sb, gain2(q_gain), gain2(k_gain), hm)


def _softmax_pv_aug(s, v_aug, extra=None):
    m = jnp.max(s, axis=-1, keepdims=True)
    if extra is not None:
        m = jnp.maximum(m, extra)
    p = jnp.exp((s - m).astype(BF16))
    o = _dot(p, v_aug)
    l = o[:, HEAD_DIM:HEAD_DIM + 1]
    if extra is not None:
        l = l + jnp.exp(extra - m)
    return o[:, 0:HEAD_DIM] * (1.0 / l)


def _window_masks():
    rows = (A_HEADS // A_KV) * WINDOW
    qi = np.arange(rows)[:, None] % WINDOW
    ki = np.arange(3 * WINDOW)[None, :]
    masks = [np.where(np.abs(ki - qi - shift * WINDOW) <= WINDOW, 0.0, NEG_INF) for shift in range(3)]
    return jnp.asarray(np.stack(masks), dtype=F32)


def _nbr_bias(rpb):
    edge = GRID_W - NA_COLS
    padded = jnp.pad(rpb, ((0, 0), (0, 0), (edge, edge)))
    toe = jnp.stack([padded[..., GRID_W - 1 - c:2 * GRID_W - 1 - c] for c in range(GRID_W)], axis=2)
    c = np.arange(GRID_W)[:, None]
    kc = np.arange(GRID_W)[None, :]
    cs = np.clip(c - NA_COLS // 2, 0, GRID_W - NA_COLS)
    toe = jnp.where((kc >= cs) & (kc < cs + NA_COLS), toe, NEG_INF)
    masked = jnp.full((C_HEADS, GRID_W, GRID_W), NEG_INF, F32)
    ntiles = GRID_ROWS // NBR_TILE_ROWS
    classes = []
    for r0 in (0, NBR_TILE_ROWS, (ntiles - 1) * NBR_TILE_ROWS):
        ws = min(max(r0 - NA_ROWS // 2, 0), GRID_ROWS - NBR_KEY_ROWS)
        per_query_row = []
        for rq in range(NBR_TILE_ROWS):
            r = r0 + rq
            rs = min(max(r - NA_ROWS // 2, 0), GRID_ROWS - NA_ROWS)
            blocks = [toe[:, ws + kr - r + NA_ROWS - 1] if rs <= ws + kr < rs + NA_ROWS else masked
                      for kr in range(NBR_KEY_ROWS)]
            per_query_row.append(jnp.stack(blocks, axis=2))
        classes.append(jnp.stack(per_query_row, axis=1))
    return jnp.stack(classes).reshape(3, C_HEADS, NBR_TILE_ROWS * GRID_W, NBR_KEY_ROWS * GRID_W)


def _mixers_kernel(sink_ref, qa_ref, ka_ref, va_ref, qb_ref, kb_ref, vb_ref, qc_ref, kc_ref, vc_ref,
                   g_ref, mask_ref, bias_ref, o_ref):
    j = pl.program_id(1)
    span = WINDOW * 3
    group_a = A_HEADS // A_KV
    rows_a = group_a * WINDOW
    nblocks = SEQ // WINDOW
    head_of_row = lax.broadcasted_iota(I32, (rows_a, 1), 0) // WINDOW
    sinks = []
    for kv in range(A_KV):
        sink = jnp.zeros((rows_a, 1), F32)
        for h in range(group_a):
            sink = jnp.where(head_of_row == h, sink_ref[kv * group_a + h], sink)
        sinks.append(sink)
    outs_a, outs_b, outs_c = {}, {}, {}

    def window_task(blk, kv):
        i = j * (MIX_TILE // WINDOW) + blk
        kstart = pl.multiple_of(jnp.clip(i * WINDOW - WINDOW, 0, SEQ - span), WINDOW)
        case = jnp.where(i == 0, 0, jnp.where(i == nblocks - 1, 2, 1))

        def score():
            qs = jnp.concatenate([qa_ref[0, blk * WINDOW:(blk + 1) * WINDOW, h * HEAD_DIM:(h + 1) * HEAD_DIM]
                                  for h in range(kv * group_a, (kv + 1) * group_a)], axis=0)
            return _dot_nt(qs, ka_ref[0, pl.ds(kstart, span), kv * HEAD_DIM:(kv + 1) * HEAD_DIM]) + mask_ref[case]

        def finish(s):
            o = _softmax_pv_aug(s, va_ref[0, pl.ds(kstart, span), kv * LANES:(kv + 1) * LANES], extra=sinks[kv])
            outs_a[blk, kv] = [o[h * WINDOW:(h + 1) * WINDOW] for h in range(group_a)]
        return score, finish

    nsub = MIX_TILE // MIX_SUB
    sub_rows = lambda sub: slice(sub * MIX_SUB, (sub + 1) * MIX_SUB)

    def dense_task(sub, h):
        kv = h // (B_HEADS // B_KV)

        def score():
            return _dot_nt(qb_ref[0, sub_rows(sub), h * HEAD_DIM:(h + 1) * HEAD_DIM],
                           kb_ref[0, :, kv * HEAD_DIM:(kv + 1) * HEAD_DIM])

        def finish(s):
            outs_b[sub, h] = _softmax_pv_aug(s, vb_ref[0, :, kv * LANES:(kv + 1) * LANES])
        return score, finish

    ntiles = GRID_ROWS // NBR_TILE_ROWS
    nkeys = NBR_KEY_ROWS * GRID_W

    def nbr_task(sub, h):
        tile = j * nsub + sub
        ws = jnp.clip(tile * NBR_TILE_ROWS - NA_ROWS // 2, 0, GRID_ROWS - NBR_KEY_ROWS)
        cls = jnp.where(tile == 0, 0, jnp.where(tile == ntiles - 1, 2, 1))
        kstart = pl.multiple_of(ws * GRID_W, GRID_W)
        cols = slice(h * HEAD_DIM, (h + 1) * HEAD_DIM)

        def score():
            return _dot_nt(qc_ref[0, sub_rows(sub), cols], kc_ref[0, pl.ds(kstart, nkeys), cols]) + bias_ref[cls, h]

        def finish(s):
            outs_c[sub, h] = _softmax_pv_aug(s, vc_ref[0, pl.ds(kstart, nkeys), h * LANES:(h + 1) * LANES])
        return score, finish

    window = [window_task(blk, kv) for blk in range(MIX_TILE // WINDOW) for kv in range(A_KV)]
    per_sub = len(window) // nsub
    tasks = []
    for sub in range(nsub):
        for t in range(per_sub):
            tasks += [dense_task(sub, t), window[sub * per_sub + t], nbr_task(sub, t)]
    scores = tasks[0][0]()
    for t, (_, finish) in enumerate(tasks):
        cur, scores = scores, (tasks[t + 1][0]() if t + 1 < len(tasks) else None)
        finish(cur)

    g = g_ref[...]
    for blk in range(MIX_TILE // WINDOW):
        o = jnp.concatenate(outs_a[blk, 0] + outs_a[blk, 1], axis=1)
        o_ref[0, blk * WINDOW:(blk + 1) * WINDOW, 0:A_W] = (_rms(o) * g[:, 0:A_W]).astype(BF16)
    for sub in range(nsub):
        o = jnp.concatenate([outs_b[sub, h] for h in range(B_HEADS)], axis=1)
        o_ref[0, sub_rows(sub), A_W:A_W + B_W] = (_rms(o) * g[:, A_W:A_W + B_W]).astype(BF16)
        o = jnp.concatenate([outs_c[sub, h] for h in range(C_HEADS)], axis=1)
        o_ref[0, sub_rows(sub), A_W + B_W:] = (_rms(o) * g[:, A_W + B_W:]).astype(BF16)


def _mixers(qa, ka, va, qb, kb, vb, qc, kc, vc, sink, bias, g_grp):
    b = qa.shape[0]
    til = lambda w: pl.BlockSpec((1, MIX_TILE, w), lambda i, j: (i, j, 0))
    seq = lambda w: pl.BlockSpec((1, SEQ, w), lambda i, j: (i, 0, 0))
    const = lambda a: pl.BlockSpec(a.shape, lambda i, j: (0,) * a.ndim)
    masks = _window_masks()
    width = A_W + B_W + C_W
    return pl.pallas_call(
        _mixers_kernel,
        grid=(b, SEQ // MIX_TILE),
        in_specs=[
            pl.BlockSpec(memory_space=pltpu.SMEM),
            til(A_W), seq(LANES), seq(A_KV * LANES),
            til(B_W), seq(LANES), seq(B_KV * LANES),
            til(C_W), seq(C_W), seq(C_HEADS * LANES),
            pl.BlockSpec((1, width), lambda i, j: (0, 0)),
            const(masks), const(bias),
        ],
        out_specs=til(width),
        out_shape=jax.ShapeDtypeStruct((b, SEQ, width), BF16),
        compiler_params=_cparams("arbitrary", "arbitrary"),
        name="mixers",
    )(sink, qa, ka, va, qb, kb, vb, qc, kc, vc, g_grp.reshape(1, width), masks, bias)


def _outproj_kernel(x_ref, m_ref, mod_ref, wo_ref, g_ref, wr_ref, x1_ref, hn_ref, aff_ref):
    parts = 2
    rows = x_ref.shape[1] // parts
    mixed = lambda p: _dot(m_ref[0, p * rows:(p + 1) * rows, :], wo_ref[0])

    nxt = mixed(0)
    for p in range(parts):
        r = slice(p * rows, (p + 1) * rows)
        attn, nxt = nxt, (mixed(p + 1) if p + 1 < parts else None)
        x1 = x_ref[0, r, :] + mod_ref[0, 2:3, :] * attn
        x1_ref[0, r, :] = x1
        hn = _rms(x1) * g_ref[...]
        hn = hn * (1.0 + mod_ref[0, 4:5, :]) + mod_ref[0, 3:4, :]
        hn_ref[r, :] = hn.astype(BF16)
        logits = _dot3(hn, wr_ref[...])
        lt = logits.T[0:N_EXPERTS, :]
        e = jnp.exp(lt - jnp.max(lt, axis=0, keepdims=True))
        aff_ref[:, r] = e * (1.0 / jnp.sum(e, axis=0, keepdims=True))


def _outproj(x, merged, mod, w_out, layer, g_ffn, w_router):
    b = x.shape[0]
    n = b * SEQ
    nt = SEQ // TOK_TILE
    tok = lambda w: pl.BlockSpec((1, TOK_TILE, w), lambda i, j: (i, j, 0))
    wr = jnp.pad(w_router, ((0, 0), (0, LANES - N_EXPERTS)))
    return pl.pallas_call(
        _outproj_kernel,
        grid=(b, nt),
        in_specs=[
            tok(D_MODEL), tok(A_W + B_W + C_W),
            pl.BlockSpec((1, 6, D_MODEL), lambda i, j: (i, 0, 0)),
            pl.BlockSpec((1, D_MODEL, D_MODEL), lambda i, j: (layer, 0, 0)),
            pl.BlockSpec((1, D_MODEL), lambda i, j: (0, 0)),
            pl.BlockSpec((D_MODEL, LANES), lambda i, j: (0, 0)),
        ],
        out_specs=[
            tok(D_MODEL),
            pl.BlockSpec((TOK_TILE, D_MODEL), lambda i, j: (i * nt + j, 0)),
            pl.BlockSpec((N_EXPERTS, TOK_TILE), lambda i, j: (0, i * nt + j)),
        ],
        out_shape=[
            jax.ShapeDtypeStruct((b, SEQ, D_MODEL), F32),
            jax.ShapeDtypeStruct((n, D_MODEL), BF16),
            jax.ShapeDtypeStruct((N_EXPERTS, n), F32),
        ],
        compiler_params=_cparams("arbitrary", "arbitrary"),
        name="outproj_router",
    )(x, merged, mod, w_out, g_ffn.reshape(1, D_MODEL), wr)


def _route_kernel(aff_ref, pos_ref, rp_ref, *, cap):
    nr = aff_ref.shape[1]

    def bisect(_, carry):
        lo, hi = carry
        mid = 0.5 * (lo + hi)
        ge = jnp.where(aff_ref[...] >= mid, 1.0, 0.0)
        cnt = jnp.sum(jnp.sum(ge, axis=1, keepdims=True), axis=2, keepdims=True)
        keep = cnt >= cap
        return jnp.where(keep, mid, lo), jnp.where(keep, hi, mid)

    lo0 = jnp.zeros((N_EXPERTS, 1, 1), F32)
    hi0 = jnp.full((N_EXPERTS, 1, 1), 2.0, F32)
    thr_lo, thr_hi = lax.fori_loop(0, BISECT_STEPS, bisect, (lo0, hi0))

    tri = jnp.where(lax.broadcasted_iota(I32, (LANES, LANES), 0) <= lax.broadcasted_iota(I32, (LANES, LANES), 1),
                    1.0, 0.0).astype(BF16)
    below = jnp.where(lax.broadcasted_iota(I32, (nr, nr), 1) < lax.broadcasted_iota(I32, (nr, nr), 0),
                      1.0, 0.0).astype(BF16)

    def ranks(mask):
        incl = _dot(mask.astype(BF16), tri)
        before = _dot(below, incl.astype(BF16))[:, LANES - 1:]
        return incl - mask + before, before

    for e in range(N_EXPERTS):
        a = aff_ref[e]
        above = jnp.where(a >= thr_hi[e], 1.0, 0.0)
        tied = jnp.where(a >= thr_lo[e], 1.0, 0.0) - above
        n_above = jnp.sum(jnp.sum(above, axis=0, keepdims=True), axis=1, keepdims=True)
        tied_rank, _ = ranks(tied)
        sel = above + tied * jnp.where(tied_rank < cap - n_above, 1.0, 0.0)
        pos, before = ranks(sel)
        pos_ref[e] = jnp.where(sel > 0.5, pos, -1.0).astype(I32)
        rp_ref[e] = before


def _block_geometry(b):
    sb = min(b, ROUTE_SEQS)
    return sb, ROUTE_TILE // sb, b // sb


def _route(aff_t, cap, b):
    n = aff_t.shape[1]
    nr = n // LANES
    sb, tt, groups = _block_geometry(b)
    aff_t = aff_t.reshape(N_EXPERTS, groups, sb, SEQ // tt, tt).transpose(0, 3, 1, 2, 4).reshape(N_EXPERTS, n)
    pos, rp = pl.pallas_call(
        functools.partial(_route_kernel, cap=cap),
        out_shape=[
            jax.ShapeDtypeStruct((N_EXPERTS, nr, LANES), I32),
            jax.ShapeDtypeStruct((N_EXPERTS, nr, 1), F32),
        ],
        compiler_params=pltpu.CompilerParams(vmem_limit_bytes=VMEM_LIMIT),
        name="route",
    )(aff_t.reshape(N_EXPERTS, nr, LANES))
    per_block = ROUTE_TILE // LANES
    offs = rp[:, ::per_block, 0].astype(I32)
    offs = jnp.concatenate([offs, jnp.full((N_EXPERTS, 1), cap, I32)], axis=1)
    return pos.reshape(N_EXPERTS, n), aff_t, offs.reshape(-1)


def _align(v):
    return pl.multiple_of(lax.shift_left(lax.shift_right_logical(v, ROW_ALIGN_SHIFT), ROW_ALIGN_SHIFT), ROW_ALIGN)


def _dispatch_kernel(offs_ref, hn_ref, pos_ref, xe_ref, stage, ostage, carry, sem, osem, *, nb, cap):
    t = pl.program_id(0)
    slot = t % 2
    off = lambda e, tt: offs_ref[e * (nb + 1) + tt]
    rows = lax.broadcasted_iota(I32, (WIN, ROUTE_TILE), 0)
    pos = pos_ref[...]
    hn = hn_ref[...].reshape(ROUTE_TILE, D_MODEL)

    def main_copy(e, tt, sl):
        return pltpu.make_async_copy(stage.at[sl, e * WIN:(e + 1) * WIN],
                                     xe_ref.at[e, pl.ds(_align(off(e, tt)), WIN)], sem.at[e])

    @pl.when(t == 0)
    def _():
        carry[...] = jnp.zeros_like(carry)

    onehot = jnp.concatenate(
        [jnp.where(pos[e:e + 1, :] - _align(off(e, t)) == rows, 1.0, 0.0) for e in range(N_EXPERTS)],
        axis=0).astype(BF16)
    stage[slot] = _dot(onehot, hn).astype(BF16)

    starts = [_align(off(e, t)) for e in range(N_EXPERTS)]
    filled = [off(e, t + 1) - starts[e] for e in range(N_EXPERTS)]
    nwin = [lax.shift_right_logical(f, WIN_SHIFT) + 1 for f in filled]
    tail = [_align(filled[e]) - (nwin[e] - 1) * WIN for e in range(N_EXPERTS)]

    for e in range(N_EXPERTS):
        first = stage[slot, e * WIN:e * WIN + ROW_ALIGN, :]
        stage[slot, e * WIN:e * WIN + ROW_ALIGN, :] = first + carry[e]

    @pl.when(t > 0)
    def _():
        for e in range(N_EXPERTS):
            main_copy(e, t - 1, 1 - slot).wait()

    for e in range(N_EXPERTS):
        main_copy(e, t, slot).start()
        row = e * WIN + jnp.where(nwin[e] == 1, tail[e], 0)
        carry[e] = stage[slot, pl.ds(pl.multiple_of(row, ROW_ALIGN), ROW_ALIGN), :]

    extra_windows = sum(nwin) - N_EXPERTS

    @pl.when(extra_windows > 0)
    def _():
        for e in range(N_EXPERTS):
            def extra(c, unused):
                lo = pl.multiple_of(starts[e] + c * WIN, ROW_ALIGN)
                win_rows = lax.broadcasted_iota(I32, (WIN, ROUTE_TILE), 0)
                oh = jnp.where(pos_ref[e:e + 1, :] - lo == win_rows, 1.0, 0.0).astype(BF16)
                ostage[...] = _dot(oh, hn_ref[...].reshape(ROUTE_TILE, D_MODEL)).astype(BF16)
                cp = pltpu.make_async_copy(ostage, xe_ref.at[e, pl.ds(lo, WIN)], osem)
                cp.start()
                cp.wait()

                @pl.when(c == nwin[e] - 1)
                def _():
                    carry[e] = ostage[pl.ds(pl.multiple_of(tail[e], ROW_ALIGN), ROW_ALIGN), :]
                return unused

            lax.fori_loop(1, nwin[e], extra, 0)

    @pl.when(t == nb - 1)
    def _():
        for e in range(N_EXPERTS):
            main_copy(e, t, slot).wait()
        stage[1 - slot, 0:WIN, :] = jnp.zeros((WIN, D_MODEL), BF16)
        pads = [pltpu.make_async_copy(stage.at[1 - slot, 0:WIN], xe_ref.at[e, cap:cap + WIN], sem.at[e])
                for e in range(N_EXPERTS)]
        for cp in pads:
            cp.start()
        for cp in pads:
            cp.wait()


def _dispatch(hn, pos, offs, cap):
    b = hn.shape[0]
    nb = b * SEQ // ROUTE_TILE
    sb, tt, groups = _block_geometry(b)
    return pl.pallas_call(
        functools.partial(_dispatch_kernel, nb=nb, cap=cap),
        grid_spec=pltpu.PrefetchScalarGridSpec(
            num_scalar_prefetch=1,
            grid=(nb,),
            in_specs=[
                pl.BlockSpec((sb, tt, D_MODEL), lambda t, o: (t % groups, t // groups, 0)),
                pl.BlockSpec((N_EXPERTS, ROUTE_TILE), lambda t, o: (0, t)),
            ],
            out_specs=pl.BlockSpec(memory_space=pl.ANY),
            scratch_shapes=[
                pltpu.VMEM((2, N_EXPERTS * WIN, D_MODEL), BF16),
                pltpu.VMEM((WIN, D_MODEL), BF16),
                pltpu.VMEM((N_EXPERTS, ROW_ALIGN, D_MODEL), BF16),
                pltpu.SemaphoreType.DMA((N_EXPERTS,)),
                pltpu.SemaphoreType.DMA(()),
            ],
        ),
        out_shape=jax.ShapeDtypeStruct((N_EXPERTS, cap + WIN, D_MODEL), BF16),
        compiler_params=_cparams("arbitrary"),
        name="dispatch",
    )(offs, hn, pos)


def _ffn_kernel(xe_ref, wg_ref, wu_ref, wd_ref, ye_ref):
    x = xe_ref[0]
    g = _dot(x, wg_ref[0, 0])
    u = _dot(x, wu_ref[0, 0])
    hid = (g * jax.nn.sigmoid(g) * u).astype(BF16)
    ye_ref[0] = _dot(hid, wd_ref[0, 0]).astype(BF16)


def _ffn(xe, w_gate, w_up, w_down, layer, cap):
    tile = min(FFN_TILE, cap)
    til = pl.BlockSpec((1, tile, D_MODEL), lambda e, j: (e, j, 0))
    wsp = pl.BlockSpec((1, 1, D_MODEL, D_MODEL), lambda e, j: (layer, e, 0, 0))
    return pl.pallas_call(
        _ffn_kernel,
        grid=(N_EXPERTS, cap // tile),
        in_specs=[til, wsp, wsp, wsp],
        out_specs=til,
        out_shape=jax.ShapeDtypeStruct((N_EXPERTS, cap, D_MODEL), BF16),
        compiler_params=_cparams("arbitrary", "arbitrary"),
        name="expert_ffn",
    )(xe, w_gate, w_up, w_down)


def _combine_kernel(offs_ref, x_ref, pos_ref, gate_ref, mod_ref, gfin_ref, ye_ref, o_ref,
                    yin, obuf, yacc, sem, osem, *, nb, cap, final):
    t = pl.program_id(0)
    slot = t % 2
    off = lambda e, tt: offs_ref[e * (nb + 1) + tt]
    base = lambda lo: pl.multiple_of(jnp.minimum(lo, cap - WIN), ROW_ALIGN)
    slots = lax.broadcasted_iota(I32, (WIN, ROUTE_TILE), 0)

    def main_copy(e, tt, sl):
        return pltpu.make_async_copy(ye_ref.at[e, pl.ds(base(_align(off(e, tt))), WIN)],
                                     yin.at[sl, e * WIN:(e + 1) * WIN], sem.at[sl, e])

    @pl.when(t == 0)
    def _():
        for e in range(N_EXPERTS):
            main_copy(e, 0, 0).start()

    ahead = jnp.minimum(t + 1, nb - 1)
    for e in range(N_EXPERTS):
        main_copy(e, ahead, 1 - slot).start()

    pos = pos_ref[...]
    gate = gate_ref[...]

    def weights(e, lo):
        p = pos[e:e + 1, :]
        p = jnp.where(p >= lo, p, -1)
        return jnp.where(p - base(lo) == slots, gate[e:e + 1, :], 0.0).astype(BF16)

    gmat = jnp.concatenate([weights(e, _align(off(e, t))) for e in range(N_EXPERTS)], axis=0)
    for e in range(N_EXPERTS):
        main_copy(e, t, slot).wait()
    yacc[...] = _dot_tn(gmat, yin[slot])

    starts = [_align(off(e, t)) for e in range(N_EXPERTS)]
    nwin = [lax.shift_right_logical(jnp.maximum(off(e, t + 1) - starts[e] - 1, 0), WIN_SHIFT) + 1
            for e in range(N_EXPERTS)]
    extra_windows = sum(nwin) - N_EXPERTS

    @pl.when(extra_windows > 0)
    def _():
        for e in range(N_EXPERTS):
            def extra(c, carry):
                lo = starts[e] + c * WIN
                cp = pltpu.make_async_copy(ye_ref.at[e, pl.ds(base(lo), WIN)], obuf, osem)
                cp.start()
                cp.wait()
                yacc[...] += _dot_tn(weights(e, lo), obuf[...])
                return carry

            lax.fori_loop(1, nwin[e], extra, 0)

    y = x_ref[...] + mod_ref[:, 5:6, :] * yacc[...].reshape(x_ref.shape)
    if final:
        y = _rms(y) * gfin_ref[...]
    o_ref[...] = y

    @pl.when(t == nb - 1)
    def _():
        for e in range(N_EXPERTS):
            main_copy(e, ahead, 1 - slot).wait()


def _combine(x1, pos, gate, mod, g_final, ye, offs, cap, final):
    b = x1.shape[0]
    nb = b * SEQ // ROUTE_TILE
    sb, tt, groups = _block_geometry(b)
    tok = pl.BlockSpec((sb, tt, D_MODEL), lambda t, o: (t % groups, t // groups, 0))
    rt = pl.BlockSpec((N_EXPERTS, ROUTE_TILE), lambda t, o: (0, t))
    return pl.pallas_call(
        functools.partial(_combine_kernel, nb=nb, cap=cap, final=final),
        grid_spec=pltpu.PrefetchScalarGridSpec(
            num_scalar_prefetch=1,
            grid=(nb,),
            in_specs=[
                tok, rt, rt,
                pl.BlockSpec((sb, 6, D_MODEL), lambda t, o: (t % groups, 0, 0)),
                pl.BlockSpec((1, D_MODEL), lambda t, o: (0, 0)),
                pl.BlockSpec(memory_space=pl.ANY),
            ],
            out_specs=tok,
            scratch_shapes=[
                pltpu.VMEM((2, N_EXPERTS * WIN, D_MODEL), BF16),
                pltpu.VMEM((WIN, D_MODEL), BF16),
                pltpu.VMEM((ROUTE_TILE, D_MODEL), F32),
                pltpu.SemaphoreType.DMA((2, N_EXPERTS)),
                pltpu.SemaphoreType.DMA(()),
            ],
        ),
        out_shape=jax.ShapeDtypeStruct((b, SEQ, D_MODEL), F32),
        compiler_params=_cparams("arbitrary"),
        name="combine",
    )(offs, x1, pos, gate, mod, g_final.reshape(1, D_MODEL), ye)


def _trunk(x, c, w_ada, b_ada, g_mix, w_in, sink, q_gain, k_gain, nbr_bias, g_grp, w_out,
           g_ffn, w_router, w_gate, w_up, w_down, g_final):
    b = x.shape[0]
    n = b * SEQ
    cap = CAPACITY_FACTOR * n // N_EXPERTS
    depth = w_ada.shape[0]
    for i in range(depth):
        mod = _ada(c, w_ada, b_ada[i], i).reshape(b, 6, D_MODEL)
        qa, ka, va, qb, kb, vb, qc, kc, vc = _inproj(x, mod, g_mix[i], w_in, i, q_gain[i], k_gain[i])
        merged = _mixers(qa, ka, va, qb, kb, vb, qc, kc, vc, sink[i], nbr_bias[i], g_grp[i])
        x1, hn, aff = _outproj(x, merged, mod, w_out, i, g_ffn[i], w_router[i])
        pos, gate, offs = _route(aff, cap, b)
        xe = _dispatch(hn.reshape(b, SEQ, D_MODEL), pos, offs, cap)
        ye = _ffn(xe, w_gate, w_up, w_down, i, cap)
        x = _combine(x1, pos, gate, mod, g_final, ye, offs, cap, final=(i == depth - 1))
    return x


def kernel(x_prompt, x_sample, c_prompt, c_sample, w_ada, b_ada, g_mix, w_in, sink, q_gain, k_gain, rpb, g_grp,
           w_out, g_ffn, w_router, w_gate, w_up, w_down, g_final):
    w_in, w_out, w_gate, w_up, w_down = (w.astype(BF16) for w in (w_in, w_out, w_gate, w_up, w_down))
    nbr_bias = [_nbr_bias(rpb[i]) for i in range(rpb.shape[0])]
    args = (w_ada, b_ada, g_mix, w_in, sink, q_gain, k_gain, nbr_bias, g_grp, w_out,
            g_ffn, w_router, w_gate, w_up, w_down, g_final)
    return (_trunk(x_prompt, c_prompt, *args), _trunk(x_sample, c_sample, *args))
```
